```python
import math
import jax, jax.numpy as jnp
from jax import lax
import numpy as np

D_MODEL = 1024
BATCH = 4
SEQ = 4096
DEPTH = 1
DEC_BATCH = 128
DEC_SEQ = 4
PAST_LEN = 2048
PAGE_SIZE = 128

GDN_HEADS = 8
GDN_HEAD_DIM = 64
GDN_WIDTH = GDN_HEADS * GDN_HEAD_DIM
CONV_WIDTH = 4
CONV_DIM = 3 * GDN_WIDTH
GDN_CHUNK = 64
FOX_HEADS = 8
FOX_HEAD_DIM = 64
FOX_WIDTH = FOX_HEADS * FOX_HEAD_DIM
Q_BLOCK = 128
N_EXPERTS = 64
N_GROUPS = 8
TOPK_GROUPS = 4
TOP_K = 8
EXPERT_FF = 256
SHARED_FF = 256
ROUTED_SCALE = 2.5
DEEPNORM_ALPHA = (2.0 * DEPTH) ** 0.25
DEEPNORM_BETA = (8.0 * DEPTH) ** -0.25
LN_EPS = 1e-5
RMS_EPS = 1e-6
IN_SIZES = (CONV_DIM, GDN_WIDTH, GDN_HEADS, GDN_HEADS, FOX_WIDTH, FOX_WIDTH, FOX_WIDTH, FOX_HEADS, D_MODEL, D_MODEL)
N_IN = sum(IN_SIZES)

kernel_name = 'hybrid_gdn_fox_moe_deepnorm_adaln_step'


def _split_cols(t, sizes):
    cuts, acc = [], 0
    for s in sizes[:-1]:
        acc += s
        cuts.append(acc)
    return jnp.split(t, cuts, axis=-1)


def _layernorm(x, g, b):
    xf = x.astype(jnp.float32)
    xc = xf - jnp.mean(xf, -1, keepdims=True)
    var = jnp.mean(xc * xc, -1, keepdims=True)
    return (xc * lax.rsqrt(var + LN_EPS) * g + b).astype(x.dtype)


def _rmsnorm(x, g):
    return x * lax.rsqrt(jnp.mean(x * x, -1, keepdims=True) + RMS_EPS) * g


def _l2norm(x):
    return x * lax.rsqrt(jnp.sum(x * x, -1, keepdims=True) + RMS_EPS)


def _causal_conv(x, buf, w):
    n = x.shape[1]
    xp = jnp.concatenate([buf.astype(x.dtype), x], axis=1)
    out = xp[:, 0:n] * w[0]
    for i in range(1, CONV_WIDTH):
        out = out + xp[:, i:i + n] * w[i]
    return jax.nn.silu(out), xp[:, n:]


def _to_chunks(t, n_chunks, c):
    t = t.reshape((t.shape[0], n_chunks, c) + t.shape[2:])
    return jnp.swapaxes(jnp.moveaxis(t, 1, 0), 2, 3)


def _gdn_chunked(q, k, v, beta, g, s0, chunk):
    nb, n, nh, dk = q.shape
    dv = v.shape[-1]
    nc = n // chunk
    q, k, v = _to_chunks(q, nc, chunk), _to_chunks(k, nc, chunk), _to_chunks(v, nc, chunk)
    beta, g = _to_chunks(beta, nc, chunk), _to_chunks(g, nc, chunk)
    big_g = jnp.cumsum(g, axis=-1)
    idx = jnp.arange(chunk)
    incl = idx[:, None] >= idx[None, :]
    strict = idx[:, None] > idx[None, :]
    gdiff = big_g[..., :, None] - big_g[..., None, :]
    dec_incl = jnp.exp(jnp.where(incl, gdiff, -jnp.inf))
    dec_strict = jnp.where(strict, dec_incl, 0.0)
    lmat = beta[..., :, None] * jnp.einsum('nbhik,nbhjk->nbhij', k, k) * dec_strict
    gam = jnp.exp(big_g)
    rhs = beta[..., None] * jnp.concatenate([v, gam[..., None] * k], axis=-1)
    sol = lax.linalg.triangular_solve(lmat, rhs, left_side=True, lower=True, unit_diagonal=True)
    w1, w2 = sol[..., :dv], sol[..., dv:]
    aqk = jnp.einsum('nbhik,nbhjk->nbhij', q, k) * dec_incl
    qg = gam[..., None] * q
    kd = jnp.exp(big_g[..., -1:] - big_g)[..., None] * k
    glast = gam[..., -1]

    def step(s, xs):
        w1c, w2c, qgc, aqkc, kdc, glc = xs
        u = w1c - jnp.einsum('bhck,bhkv->bhcv', w2c, s)
        o = jnp.einsum('bhck,bhkv->bhcv', qgc, s) + jnp.einsum('bhij,bhjv->bhiv', aqkc, u)
        s = glc[..., None, None] * s + jnp.einsum('bhck,bhcv->bhkv', kdc, u)
        return s, o

    s_fin, o = lax.scan(step, s0, (w1, w2, qg, aqk, kd, glast))
    o = jnp.moveaxis(jnp.swapaxes(o, 2, 3), 0, 1).reshape(nb, n, nh, dv)
    return o, s_fin


def _fox_prompt(q, k, v, logf):
    nb, n, nh, hd = q.shape
    scale = hd ** -0.5
    big_f = jnp.cumsum(logf, axis=1)
    nblk = n // Q_BLOCK
    qb = jnp.swapaxes(q.reshape(nb, nblk, Q_BLOCK, nh, hd), 0, 1)
    fb = jnp.swapaxes(big_f.reshape(nb, nblk, Q_BLOCK, nh), 0, 1)
    pos = jnp.arange(n)
    posb = pos.reshape(nblk, Q_BLOCK)
    fk = jnp.transpose(big_f, (0, 2, 1))[:, :, None, :]

    def one_block(args):
        qc, fc, pc = args
        s = jnp.einsum('bqhd,bkhd->bhqk', qc, k) * scale + jnp.transpose(fc, (0, 2, 1))[..., None] - fk
        s = jnp.where(pc[:, None] >= pos[None, :], s, -jnp.inf)
        p = jax.nn.softmax(s, axis=-1)
        return jnp.einsum('bhqk,bkhd->bqhd', p, v)

    o = lax.map(one_block, (qb, fb, posb))
    return jnp.swapaxes(o, 0, 1).reshape(nb, n, nh, hd)


def _fox_sample(q, k, v, logf, k_past, v_past, logf_past):
    hd = q.shape[-1]
    n = q.shape[1]
    scale = hd ** -0.5
    k_past = k_past.astype(jnp.float32)
    v_past = v_past.astype(jnp.float32)
    lfp = logf_past.astype(jnp.float32)
    suffix = lax.cumsum(lfp, axis=1, reverse=True) - lfp
    fn = jnp.transpose(jnp.cumsum(logf, axis=1), (0, 2, 1))
    s_past = (jnp.einsum('blhd,bphd->bhlp', q, k_past) * scale + fn[..., None]
              + jnp.transpose(suffix, (0, 2, 1))[:, :, None, :])
    idx = jnp.arange(n)
    s_new = jnp.einsum('blhd,bmhd->bhlm', q, k) * scale + fn[..., :, None] - fn[..., None, :]
    s_new = jnp.where(idx[:, None] >= idx[None, :], s_new, -jnp.inf)
    p = jax.nn.softmax(jnp.concatenate([s_past, s_new], axis=-1), axis=-1)
    npast = k_past.shape[1]
    return (jnp.einsum('bhlp,bphd->blhd', p[..., :npast], v_past)
            + jnp.einsum('bhlm,bmhd->blhd', p[..., npast:], v))


def _moe(h, w_router, router_bias, w_gate, w_up, w_down, ws_gate, ws_up, ws_down):
    shp = h.shape
    x = h.reshape(-1, shp[-1])
    n_tok = x.shape[0]
    s = jax.nn.sigmoid((x @ w_router).astype(jnp.float32))
    sb = s + router_bias.astype(jnp.float32)
    grp = sb.reshape(n_tok, N_GROUPS, N_EXPERTS // N_GROUPS)
    gscore = jnp.sum(lax.top_k(grp, 2)[0], axis=-1)
    _, gidx = lax.top_k(gscore, TOPK_GROUPS)
    gmask = jnp.sum(jax.nn.one_hot(gidx, N_GROUPS), axis=1) > 0
    emask = jnp.repeat(gmask, N_EXPERTS // N_GROUPS, axis=1)
    _, eidx = lax.top_k(jnp.where(emask, sb, -jnp.inf), TOP_K)
    wsel = jnp.take_along_axis(s, eidx, axis=1)
    wsel = wsel / jnp.sum(wsel, axis=-1, keepdims=True) * ROUTED_SCALE
    gates = jnp.einsum('tk,tke->te', wsel, jax.nn.one_hot(eidx, N_EXPERTS, dtype=wsel.dtype))

    def expert(acc, xs):
        wg, wu, wd, ge = xs
        y = (jax.nn.silu(x @ wg) * (x @ wu)) @ wd
        return acc + ge[:, None] * y.astype(jnp.float32), None

    routed, _ = lax.scan(expert, jnp.zeros((n_tok, shp[-1]), jnp.float32), (w_gate, w_up, w_down, gates.T))
    shared = (jax.nn.silu(x @ ws_gate) * (x @ ws_up)) @ ws_down
    return (routed + shared.astype(jnp.float32)).reshape(shp).astype(h.dtype)


def _layer(x, c, p, conv_buf, ssm0, fox_past):
    (w_ada, b_ada, w_in, conv_w, a_log, dt_bias, norm_g, f_bias, w_ba, w_bb, w_out, ln1_g, ln1_b,
     w_router, router_bias, w_gate, w_up, w_down, ws_gate, ws_up, ws_down, ln2_g, ln2_b) = p
    f32 = jnp.float32
    nb, n, _ = x.shape
    mod = (c @ w_ada + b_ada)[:, None, :]
    sh1, sc1, gt1, sh2, sc2, gt2 = jnp.split(mod, 6, axis=-1)
    h = x * (1.0 + sc1) + sh1
    (qkv_a, z_a, b_a, a_a, q_b, k_b, v_b, f_b, g_a, g_b) = _split_cols(h @ w_in, IN_SIZES)
    qkv_a, conv_new = _causal_conv(qkv_a, conv_buf, conv_w)
    q_a, k_a, v_a = jnp.split(qkv_a.astype(f32), 3, axis=-1)
    hs = (nb, n, GDN_HEADS, GDN_HEAD_DIM)
    q_a = _l2norm(q_a.reshape(hs)) * GDN_HEAD_DIM ** -0.5
    k_a = _l2norm(k_a.reshape(hs))
    v_a = v_a.reshape(hs)
    beta = jax.nn.sigmoid(b_a.astype(f32))
    g = -jnp.exp(a_log.astype(f32)) * jax.nn.softplus(a_a.astype(f32) + dt_bias.astype(f32))
    chunk = GDN_CHUNK if n % GDN_CHUNK == 0 else n
    o_a, ssm_new = _gdn_chunked(q_a, k_a, v_a, beta, g, ssm0.astype(f32), chunk)
    o_a = _rmsnorm(o_a, norm_g.astype(f32)) * jax.nn.silu(z_a.astype(f32).reshape(hs))
    o_a = o_a.reshape(nb, n, GDN_WIDTH).astype(x.dtype)
    fs = (nb, n, FOX_HEADS, FOX_HEAD_DIM)
    logf = jax.nn.log_sigmoid(f_b.astype(f32) + f_bias.astype(f32))
    q_b = q_b.astype(f32).reshape(fs)
    k_b = k_b.astype(f32).reshape(fs)
    v_b = v_b.astype(f32).reshape(fs)
    if fox_past is None:
        o_b = _fox_prompt(q_b, k_b, v_b, logf)
    else:
        o_b = _fox_sample(q_b, k_b, v_b, logf, fox_past[0], fox_past[1], fox_past[2])
    o_b = o_b.reshape(nb, n, FOX_WIDTH).astype(x.dtype)
    merged = jax.nn.sigmoid(g_a) * (o_a @ w_ba) + jax.nn.sigmoid(g_b) * (o_b @ w_bb)
    x = _layernorm(DEEPNORM_ALPHA * x + gt1 * (merged @ w_out), ln1_g, ln1_b)
    h2 = x * (1.0 + sc2) + sh2
    ffn = _moe(h2, w_router, router_bias, w_gate, w_up, w_down, ws_gate, ws_up, ws_down)
    x = _layernorm(DEEPNORM_ALPHA * x + gt2 * ffn, ln2_g, ln2_b)
    return x, (k_b.astype(x.dtype), v_b.astype(x.dtype), logf.astype(x.dtype), ssm_new, conv_new)


def setup_inputs(seed: int = 0) -> dict:
    key = jax.random.key(seed)
    ks = iter(jax.random.split(key, 40))
    f32 = jnp.float32

    def nrm(shape, scale=1.0):
        return jax.random.normal(next(ks), shape, f32) * scale

    n_pages = PAST_LEN // PAGE_SIZE
    n_used = DEC_BATCH * n_pages
    n_phys = n_used + max(n_used // 4, 1)
    x_prompt = nrm((BATCH, SEQ, D_MODEL))
    x_sample = nrm((DEC_BATCH, DEC_SEQ, D_MODEL))
    c_prompt = nrm((BATCH, D_MODEL))
    c_sample = nrm((DEC_BATCH, D_MODEL))
    cache_k = nrm((DEPTH, n_phys, PAGE_SIZE, FOX_HEADS, FOX_HEAD_DIM))
    cache_v = nrm((DEPTH, n_phys, PAGE_SIZE, FOX_HEADS, FOX_HEAD_DIM))
    cache_logf = jax.nn.log_sigmoid(nrm((DEPTH, n_phys, PAGE_SIZE, FOX_HEADS)) + 8.0)
    state_ssm = nrm((DEPTH, DEC_BATCH, GDN_HEADS, GDN_HEAD_DIM, GDN_HEAD_DIM), 0.3)
    state_conv = nrm((DEPTH, DEC_BATCH, CONV_WIDTH - 1, CONV_DIM))
    page_table = jax.random.permutation(next(ks), n_phys)[:n_used].reshape(DEC_BATCH, n_pages).astype(jnp.int32)
    d = D_MODEL
    w_ada = nrm((DEPTH, d, 6 * d), 0.5 * d ** -0.5)
    b_ada = nrm((DEPTH, 6 * d), 0.02)
    w_in = nrm((DEPTH, d, N_IN), d ** -0.5)
    conv_w = nrm((DEPTH, CONV_WIDTH, CONV_DIM), CONV_WIDTH ** -0.5)
    gdn_a_log = jnp.log(jax.random.uniform(next(ks), (DEPTH, GDN_HEADS), f32, 1.0, 16.0))
    dt = jnp.exp(jax.random.uniform(next(ks), (DEPTH, GDN_HEADS), f32, math.log(1e-3), math.log(1e-1)))
    gdn_dt_bias = dt + jnp.log(-jnp.expm1(-dt))
    gdn_norm_g = 1.0 + nrm((DEPTH, GDN_HEAD_DIM), 0.02)
    fox_f_bias = jax.random.uniform(next(ks), (DEPTH, FOX_HEADS), f32, 1.0, 4.0)
    w_branch_a = nrm((DEPTH, GDN_WIDTH, d), GDN_WIDTH ** -0.5 * DEEPNORM_BETA)
    w_branch_b = nrm((DEPTH, FOX_WIDTH, d), FOX_WIDTH ** -0.5 * DEEPNORM_BETA)
    w_out = nrm((DEPTH, d, d), d ** -0.5 * DEEPNORM_BETA)
    ln1_g = 1.0 + nrm((DEPTH, d), 0.02)
    ln1_b = nrm((DEPTH, d), 0.02)
    w_router = nrm((DEPTH, d, N_EXPERTS), d ** -0.5)
    router_bias = nrm((DEPTH, N_EXPERTS), 0.01)
    w_gate = nrm((DEPTH, N_EXPERTS, d, EXPERT_FF), d ** -0.5)
    w_up = nrm((DEPTH, N_EXPERTS, d, EXPERT_FF), d ** -0.5)
    w_down = nrm((DEPTH, N_EXPERTS, EXPERT_FF, d), EXPERT_FF ** -0.5 * DEEPNORM_BETA)
    ws_gate = nrm((DEPTH, d, SHARED_FF), d ** -0.5)
    ws_up = nrm((DEPTH, d, SHARED_FF), d ** -0.5)
    ws_down = nrm((DEPTH, SHARED_FF, d), SHARED_FF ** -0.5 * DEEPNORM_BETA)
    ln2_g = 1.0 + nrm((DEPTH, d), 0.02)
    ln2_b = nrm((DEPTH, d), 0.02)
    return dict(x_prompt=x_prompt, x_sample=x_sample, c_prompt=c_prompt, c_sample=c_sample,
                cache_k=cache_k, cache_v=cache_v, cache_logf=cache_logf, state_ssm=state_ssm,
                state_conv=state_conv, page_table=page_table, w_ada=w_ada, b_ada=b_ada, w_in=w_in,
                conv_w=conv_w, gdn_a_log=gdn_a_log, gdn_dt_bias=gdn_dt_bias, gdn_norm_g=gdn_norm_g,
                fox_f_bias=fox_f_bias, w_branch_a=w_branch_a, w_branch_b=w_branch_b, w_out=w_out,
                ln1_g=ln1_g, ln1_b=ln1_b, w_router=w_router, router_bias=router_bias, w_gate=w_gate,
                w_up=w_up, w_down=w_down, ws_gate=ws_gate, ws_up=ws_up, ws_down=ws_down,
                ln2_g=ln2_g, ln2_b=ln2_b)


def reference(x_prompt, x_sample, c_prompt, c_sample, cache_k, cache_v, cache_logf, state_ssm,
              state_conv, page_table, w_ada, b_ada, w_in, conv_w, gdn_a_log, gdn_dt_bias, gdn_norm_g,
              fox_f_bias, w_branch_a, w_branch_b, w_out, ln1_g, ln1_b, w_router, router_bias, w_gate,
              w_up, w_down, ws_gate, ws_up, ws_down, ln2_g, ln2_b):
    n_seq, n_pages = page_table.shape
    past = n_pages * cache_k.shape[2]
    bp = x_prompt.shape[0]
    yp, ys = x_prompt, x_sample
    kp_l, vp_l, fp_l, sp_l, cp_l = [], [], [], [], []
    ks_l, vs_l, fs_l, ss_l, cs_l = [], [], [], [], []
    for l in range(DEPTH):
        p = (w_ada[l], b_ada[l], w_in[l], conv_w[l], gdn_a_log[l], gdn_dt_bias[l], gdn_norm_g[l],
             fox_f_bias[l], w_branch_a[l], w_branch_b[l], w_out[l], ln1_g[l], ln1_b[l], w_router[l],
             router_bias[l], w_gate[l], w_up[l], w_down[l], ws_gate[l], ws_up[l], ws_down[l],
             ln2_g[l], ln2_b[l])
        conv0 = jnp.zeros((bp, CONV_WIDTH - 1, CONV_DIM), yp.dtype)
        ssm0 = jnp.zeros((bp, GDN_HEADS, GDN_HEAD_DIM, GDN_HEAD_DIM), jnp.float32)
        yp, (kp, vp, fp, sp, cp) = _layer(yp, c_prompt, p, conv0, ssm0, None)
        k_past = jnp.take(cache_k[l], page_table, axis=0).reshape(n_seq, past, FOX_HEADS, FOX_HEAD_DIM)
        v_past = jnp.take(cache_v[l], page_table, axis=0).reshape(n_seq, past, FOX_HEADS, FOX_HEAD_DIM)
        f_past = jnp.take(cache_logf[l], page_table, axis=0).reshape(n_seq, past, FOX_HEADS)
        ys, (k_s, v_s, f_s, s_s, c_s) = _layer(ys, c_sample, p, state_conv[l], state_ssm[l],
                                               (k_past, v_past, f_past))
        kp_l.append(kp); vp_l.append(vp); fp_l.append(fp); sp_l.append(sp); cp_l.append(cp)
        ks_l.append(k_s); vs_l.append(v_s); fs_l.append(f_s); ss_l.append(s_s); cs_l.append(c_s)
    return (yp, ys, jnp.stack(kp_l), jnp.stack(vp_l), jnp.stack(fp_l), jnp.stack(sp_l), jnp.stack(cp_l),
            jnp.stack(ks_l), jnp.stack(vs_l), jnp.stack(fs_l), jnp.stack(ss_l), jnp.stack(cs_l))
```

```python
import functools

import jax
import jax.numpy as jnp
from jax import lax
from jax.experimental import pallas as pl
from jax.experimental.pallas import tpu as pltpu

F32 = jnp.float32
BF16 = jnp.bfloat16

V7X_VMEM_BYTES = 64 * 1024 * 1024
VMEM_LIMIT_BYTES = V7X_VMEM_BYTES - 8 * 1024 * 1024
LANES = 128
SUBLANES = 8

N_GROUPS = 8
TOPK_GROUPS = 4
TOP_K = 8
ROUTED_SCALE = 2.5
CONV_WIDTH = 4
GDN_CHUNK = 64
LN_EPS = 1e-5
RMS_EPS = 1e-6
LOG2E = 1.4426950408889634


def _cparams(*sem):
    return pltpu.CompilerParams(dimension_semantics=sem, vmem_limit_bytes=VMEM_LIMIT_BYTES)


def _dot(a, b):
    return jnp.dot(a, b, preferred_element_type=F32)


def _dot_nt(a, b):
    return lax.dot_general(a, b, (((1,), (1,)), ((), ())), preferred_element_type=F32)


def _dot_tn(a, b):
    return lax.dot_general(a, b, (((0,), (0,)), ((), ())), preferred_element_type=F32)


def _split2(x):
    hi = x.astype(BF16)
    lo = (x - hi.astype(F32)).astype(BF16)
    return hi, lo


def _split3(x):
    hi = x.astype(BF16)
    r = x - hi.astype(F32)
    mid = r.astype(BF16)
    lo = (r - mid.astype(F32)).astype(BF16)
    return hi, mid, lo


def _dot_sel(x, sel, parts):
    r = x.shape[0]
    pieces = _split3(x) if parts == 3 else _split2(x)
    y = _dot(jnp.concatenate(pieces, axis=0), sel)
    out = y[:r]
    for i in range(1, parts):
        out = out + y[i * r:(i + 1) * r]
    return out


def _block_diag(x, half):
    lane = lax.broadcasted_iota(jnp.int32, x.shape, 1)
    left = lane < half
    zero = jnp.zeros_like(x)
    return jnp.concatenate([jnp.where(left, x, zero), jnp.where(left, zero, x)], axis=0)


def _layernorm(r, g, b):
    mu = jnp.mean(r, axis=-1, keepdims=True)
    rc = r - mu
    var = jnp.mean(rc * rc, axis=-1, keepdims=True)
    return rc * lax.rsqrt(var + LN_EPS) * g + b


def _ada_kernel(c_ref, w_ref, b_ref, o_ref):
    o_ref[...] = _dot(c_ref[...].astype(BF16), w_ref[...].astype(BF16)) + b_ref[...]


def _ada(c, w, b, tn=512):
    m, d = c.shape
    n = w.shape[1]
    return pl.pallas_call(
        _ada_kernel,
        grid=(n // tn,),
        in_specs=[pl.BlockSpec((m, d), lambda j: (0, 0)),
                  pl.BlockSpec((d, tn), lambda j: (0, j)),
                  pl.BlockSpec((1, tn), lambda j: (0, j))],
        out_specs=pl.BlockSpec((m, tn), lambda j: (0, j)),
        out_shape=jax.ShapeDtypeStruct((m, n), F32),
        compiler_params=_cparams("arbitrary"),
        name="ada",
    )(c, w, b.reshape(1, n))


def _inproj_kernel(x_ref, sc_ref, sh_ref, w_ref, gates_ref, qkv_ref, z_ref, qb_ref, kb_ref, vb_ref,
                   kf_ref, vf_ref, small_ref, *, d, conv_dim, width, q_scale, per_row):
    sc = sc_ref[...] if per_row else sc_ref[0]
    sh = sh_ref[...] if per_row else sh_ref[0]
    h = (x_ref[...] * (1.0 + sc) + sh).astype(BF16)
    off = [0]

    def mm(n):
        a = off[0]
        off[0] = a + n
        return _dot(h, w_ref[:, a:a + n])

    gates_ref[...] = mm(2 * d)
    qkv_ref[...] = mm(conv_dim)
    z_ref[...] = mm(width)
    qb_ref[...] = (mm(width) * q_scale).astype(BF16)
    k = mm(width)
    kf_ref[...] = k
    kb_ref[...] = k.astype(BF16)
    v = mm(width)
    vf_ref[...] = v
    vb_ref[...] = v.astype(BF16)
    small_ref[...] = mm(LANES)


def _inproj(x, sc, sh, w_perm, *, seq_len, conv_dim, width, q_scale, tm):
    t, d = x.shape
    n = w_perm.shape[1]
    per_row = sc.ndim == 2
    if per_row:
        mod_spec = pl.BlockSpec((tm, d), lambda i: (i, 0))
    else:
        steps_per_seq = seq_len // tm
        mod_spec = pl.BlockSpec((1, 1, d), lambda i: (i // steps_per_seq, 0, 0))
    row = lambda w_: pl.BlockSpec((tm, w_), lambda i: (i, 0))
    outs = [(2 * d, F32), (conv_dim, F32), (width, F32), (width, BF16), (width, BF16), (width, BF16),
            (width, F32), (width, F32), (LANES, F32)]
    return pl.pallas_call(
        functools.partial(_inproj_kernel, d=d, conv_dim=conv_dim, width=width, q_scale=q_scale,
                          per_row=per_row),
        grid=(t // tm,),
        in_specs=[row(d), mod_spec, mod_spec,
                  pl.BlockSpec((d, n), lambda i: (0, 0), pipeline_mode=pl.Buffered(1))],
        out_specs=[row(w_) for w_, _ in outs],
        out_shape=[jax.ShapeDtypeStruct((t, w_), dt) for w_, dt in outs],
        compiler_params=_cparams("arbitrary"),
        name="inproj",
    )(x, sc, sh, w_perm)


def _conv_kernel(x_ref, prev_ref, hist_ref, cw_ref, bsum_ref, q_ref, k_ref, v_ref, xbuf, *, rows, width,
                 q_scale):
    t = pl.program_id(1)
    x = x_ref[...]
    halo = jnp.where(t == 0, hist_ref[0], prev_ref[...])
    xbuf[0:SUBLANES, :] = halo
    xbuf[SUBLANES:SUBLANES + rows, :] = x
    cw = cw_ref[...]
    conv = x * cw[3:4]
    for i in range(CONV_WIDTH - 1):
        shift = CONV_WIDTH - 1 - i
        conv = conv + xbuf[pl.ds(SUBLANES - shift, rows), :] * cw[i:i + 1]
    y = conv * jax.nn.sigmoid(conv)
    q = y[:, :width]
    k = y[:, width:2 * width]
    bsum = bsum_ref[...]
    q_ref[...] = q * lax.rsqrt(_dot_sel(q * q, bsum, 2) + RMS_EPS) * q_scale
    k_ref[...] = k * lax.rsqrt(_dot_sel(k * k, bsum, 2) + RMS_EPS)
    v_ref[...] = y[:, 2 * width:]


def _conv(x, hist, cw, bsum, *, seq_rows, rows, width, q_scale):
    t, cd = x.shape
    nb = t // seq_rows
    tiles = seq_rows // rows
    rb = rows // SUBLANES
    outs = [jax.ShapeDtypeStruct((t, width), F32)] * 3
    return pl.pallas_call(
        functools.partial(_conv_kernel, rows=rows, width=width, q_scale=q_scale),
        grid=(nb, tiles),
        in_specs=[pl.BlockSpec((rows, cd), lambda b, i: (b * tiles + i, 0)),
                  pl.BlockSpec((SUBLANES, cd), lambda b, i: (jnp.maximum((b * tiles + i) * rb - 1, 0), 0)),
                  pl.BlockSpec((1, SUBLANES, cd), lambda b, i: (b, 0, 0)),
                  pl.BlockSpec((CONV_WIDTH, cd), lambda b, i: (0, 0)),
                  pl.BlockSpec((width, width), lambda b, i: (0, 0))],
        out_specs=[pl.BlockSpec((rows, width), lambda b, i: (b * tiles + i, 0))] * 3,
        out_shape=outs,
        scratch_shapes=[pltpu.VMEM((rows + SUBLANES, cd), F32)],
        compiler_params=_cparams("arbitrary", "arbitrary"),
        name="gdn_conv",
    )(x, x, hist, cw, bsum)


def _sbs(a, b, half):
    return _dot(a.astype(BF16), _block_diag(b.astype(BF16), half))


def _gdn_prep_kernel(q_ref, k_ref, v_ref, aux_ref, grow_ref, esel_ref, w1_ref, w2_ref, qg_ref, kd_ref,
                     aqk_ref, gl_ref, beta_s, gc_s, *, chunk, n_chunks, width, n_live):
    half = LANES // 2
    ex = _dot_sel(aux_ref[...], esel_ref[...], 3)
    beta_s[...] = ex[:, :width]
    gc_s[...] = ex[:, width:]
    ii = lax.broadcasted_iota(jnp.int32, (chunk, LANES), 0)
    jj = lax.broadcasted_iota(jnp.int32, (chunk, LANES), 1) & (half - 1)
    incl = ii >= jj
    strict = ii > jj
    eye = (ii == jj).astype(F32)

    streams = [(c, p) for c in range(n_chunks) for p in range(width // LANES)]
    st = []
    for c, p in streams:
        rows = slice(c * chunk, (c + 1) * chunk)
        ls = slice(p * LANES, (p + 1) * LANES)
        kp, qp, vp = k_ref[rows, ls], q_ref[rows, ls], v_ref[rows, ls]
        bb, gb = beta_s[rows, ls], gc_s[rows, ls]
        dec = jnp.where(incl, jnp.exp(jnp.where(incl, gb - grow_ref[c:c + 1, p, :], 0.0)), 0.0)
        kb = kp.astype(BF16)
        kbd = _block_diag(kb, half)
        lm = jnp.where(strict, bb * _dot_nt(kb, kbd) * dec, 0.0)
        st.append(dict(kp=kp, qp=qp, vp=vp, bb=bb, gb=gb, dec=dec, lm=lm, qk=_dot_nt(qp.astype(BF16), kbd),
                       dinv=eye - jnp.where((ii ^ jj) == 1, lm, 0.0)))
    s = 2
    while s < min(chunk, n_live):
        sel = ((ii // (2 * s)) == (jj // (2 * s))) & ((ii & s) != 0) & ((jj & s) == 0)
        tmp = [_sbs(x["dinv"], jnp.where(sel, x["lm"], 0.0), half) for x in st]
        for x, t_ in zip(st, tmp):
            x["dinv"] = x["dinv"] - _sbs(t_, x["dinv"], half)
        s *= 2
    for (c, p), x in zip(streams, st):
        rows = slice(c * chunk, (c + 1) * chunk)
        ls = slice(p * LANES, (p + 1) * LANES)
        gb, kp, bb = x["gb"], x["kp"], x["bb"]
        gam = jnp.exp(gb)
        rhs = jnp.concatenate([_block_diag((bb * x["vp"]).astype(BF16), half),
                               _block_diag((bb * gam * kp).astype(BF16), half)], axis=1)
        w = _dot(x["dinv"].astype(BF16), rhs)
        glast = gb[chunk - 1:chunk, :]
        w1_ref[rows, ls] = w[:, :LANES]
        w2_ref[rows, ls] = w[:, LANES:].astype(BF16)
        qg_ref[rows, ls] = (x["qp"] * gam).astype(BF16)
        kd_ref[rows, ls] = (kp * jnp.exp(glast - gb)).astype(BF16)
        aqk_ref[rows, ls] = (x["qk"] * x["dec"]).astype(BF16)
        gl_ref[c, :, ls] = jnp.exp(glast)


def _gdn_prep(q, k, v, aux, grow, esel, *, chunk, n_chunks, n_live):
    t, width = q.shape
    rows = chunk * n_chunks
    nct = t // chunk
    row = lambda dt: (pl.BlockSpec((rows, width), lambda i: (i, 0)), jax.ShapeDtypeStruct((t, width), dt))
    outs = [row(F32), row(BF16), row(BF16), row(BF16), row(BF16),
            (pl.BlockSpec((n_chunks, 1, width), lambda i: (i, 0, 0)),
             jax.ShapeDtypeStruct((nct, 1, width), F32))]
    return pl.pallas_call(
        functools.partial(_gdn_prep_kernel, chunk=chunk, n_chunks=n_chunks, width=width, n_live=n_live),
        grid=(t // rows,),
        in_specs=[pl.BlockSpec((rows, width), lambda i: (i, 0))] * 3 + [
            pl.BlockSpec((rows, LANES), lambda i: (i, 0)),
            pl.BlockSpec((n_chunks, width // LANES, LANES), lambda i: (i, 0, 0)),
            pl.BlockSpec((LANES, 2 * width), lambda i: (0, 0))],
        out_specs=[o[0] for o in outs],
        out_shape=[o[1] for o in outs],
        scratch_shapes=[pltpu.VMEM((rows, width), F32), pltpu.VMEM((rows, width), F32)],
        compiler_params=_cparams("arbitrary"),
        name="gdn_prep",
    )(q, k, v, aux, grow, esel)


def _gdn_state_kernel(w1_ref, w2_ref, qg_ref, kd_ref, aqk_ref, gl_ref, z_ref, ng_ref, s0_ref, bones_ref,
                      o_ref, sout_ref, s_sc, *, chunk, n_chunks, n_seq, width, hd):
    half = LANES // 2
    n_pairs = width // LANES
    c = pl.program_id(1)
    streams = [(b, p) for b in range(n_seq) for p in range(n_pairs)]

    @pl.when(c == 0)
    def _():
        for i, (b, p) in enumerate(streams):
            pair = jnp.concatenate([s0_ref[b, 2 * p], s0_ref[b, 2 * p + 1]], axis=1)
            s_sc[i] = _block_diag(pair, half)

    r_i = lax.broadcasted_iota(jnp.int32, (LANES, LANES), 0)
    c_i = lax.broadcasted_iota(jnp.int32, (LANES, LANES), 1)
    same_head = (r_i < half) == (c_i < half)
    bones = bones_ref[...]
    for cc in range(n_chunks):
        rows = slice(cc * chunk, (cc + 1) * chunk)
        ins = []
        for i, (b, p) in enumerate(streams):
            ls = slice(p * LANES, (p + 1) * LANES)
            ins.append((s_sc[i], w1_ref[b, rows, ls], w2_ref[b, rows, ls], qg_ref[b, rows, ls],
                        kd_ref[b, rows, ls], aqk_ref[b, rows, ls], gl_ref[b, cc, :, ls], z_ref[b, rows, ls]))
        outs = []
        for (b, p), (s, w1, w2, qg, kd, aqk, gl, zz) in zip(streams, ins):
            ls = slice(p * LANES, (p + 1) * LANES)
            sb = s.astype(BF16)
            u = w1 - _dot(w2, sb)
            ub = u.astype(BF16)
            o = _dot(qg, sb) + _dot(aqk, _block_diag(ub, half))
            s_new = jnp.where(same_head, gl * s + _dot_tn(kd, ub), 0.0)
            ms = _dot_sel(o * o, bones, 2) * (1.0 / hd)
            on = o * lax.rsqrt(ms + RMS_EPS) * ng_ref[:, ls] * (zz * jax.nn.sigmoid(zz))
            outs.append((s_new, on.astype(BF16)))
        for i, (b, p) in enumerate(streams):
            s_sc[i] = outs[i][0]
            o_ref[b, rows, p * LANES:(p + 1) * LANES] = outs[i][1]

    @pl.when(c == pl.num_programs(1) - 1)
    def _():
        for i, (b, p) in enumerate(streams):
            s = s_sc[i]
            sout_ref[b, 2 * p] = s[:half, :half]
            sout_ref[b, 2 * p + 1] = s[half:, half:]


def _gdn_state(w1, w2, qg, kd, aqk, gl, z, ng, s0, bones, *, chunk, n_chunks, n_seq, seq_rows, hd):
    t, width = w1.shape
    nb = t // seq_rows
    rows = chunk * n_chunks
    nh = s0.shape[1]
    as_seq = lambda a: a.reshape(nb, seq_rows, width)
    row = pl.BlockSpec((n_seq, rows, width), lambda b, i: (b, i, 0))
    state = pl.BlockSpec((n_seq, nh, hd, hd), lambda b, i: (b, 0, 0, 0))
    o, s_out = pl.pallas_call(
        functools.partial(_gdn_state_kernel, chunk=chunk, n_chunks=n_chunks, n_seq=n_seq, width=width, hd=hd),
        grid=(nb // n_seq, seq_rows // rows),
        in_specs=[row, row, row, row, row,
                  pl.BlockSpec((n_seq, n_chunks, 1, width), lambda b, i: (b, i, 0, 0)),
                  row,
                  pl.BlockSpec((1, width), lambda b, i: (0, 0)),
                  state,
                  pl.BlockSpec((LANES, LANES), lambda b, i: (0, 0))],
        out_specs=[row, state],
        out_shape=[jax.ShapeDtypeStruct((nb, seq_rows, width), BF16), jax.ShapeDtypeStruct(s0.shape, F32)],
        scratch_shapes=[pltpu.VMEM((n_seq * (width // LANES), LANES, LANES), F32)],
        compiler_params=_cparams("arbitrary", "arbitrary"),
        name="gdn_state",
    )(as_seq(w1), as_seq(w2), as_seq(qg), as_seq(kd), as_seq(aqk),
      gl.reshape(nb, seq_rows // chunk, 1, width), as_seq(z), ng, s0, bones)
    return o.reshape(t, width), s_out


def _fox_prompt_kernel(q_ref, k_ref, v_ref, fq_ref, fk_ref, o_ref, m_sc, l_sc, acc_sc, *, g, ng, hd, tq, tk):
    qi = pl.program_id(1)
    gw = g * hd
    lane = lax.broadcasted_iota(jnp.int32, (tq, gw), 1) // hd
    q_bd, fq_col = [], []
    for gi in range(ng):
        q = q_ref[0, :, gi * gw:(gi + 1) * gw]
        zero = jnp.zeros_like(q)
        q_bd.append(jnp.concatenate([jnp.where(lane == i, q, zero) for i in range(g)], axis=0))
        fq = fq_ref[0, gi]
        fq_col.append(jnp.concatenate([fq[:, i:i + 1] for i in range(g)], axis=0))
    m_sc[...] = jnp.full(m_sc.shape, -jnp.inf, F32)
    l_sc[...] = jnp.zeros(l_sc.shape, F32)
    acc_sc[...] = jnp.zeros(acc_sc.shape, F32)

    def block(j, masked):
        ks = pl.ds(pl.multiple_of(j * tk, tk), tk)
        if masked:
            rr = lax.broadcasted_iota(jnp.int32, (tq, tk), 0) + qi * tq
            cc = lax.broadcasted_iota(jnp.int32, (tq, tk), 1) + j * tk
            ok = jnp.concatenate([rr >= cc] * g, axis=0)
        old = [(m_sc[gi], l_sc[gi], acc_sc[gi]) for gi in range(ng)]
        new = []
        for gi in range(ng):
            kb = k_ref[0, ks, gi * gw:(gi + 1) * gw]
            vb = v_ref[0, ks, gi * gw:(gi + 1) * gw]
            fk = fk_ref[0, gi, :, ks]
            fk_full = jnp.concatenate([jnp.broadcast_to(fk[i:i + 1, :], (tq, tk)) for i in range(g)], axis=0)
            s = _dot_nt(q_bd[gi], kb) - fk_full
            if masked:
                s = jnp.where(ok, s, -jnp.inf)
            m_old, l_old, acc_old = old[gi]
            m_new = jnp.maximum(m_old, jnp.max(s, axis=1, keepdims=True) + fq_col[gi])
            alpha = jnp.exp2(m_old - m_new)
            p = jnp.exp2(s - (m_new - fq_col[gi]))
            new.append((m_new, alpha * l_old + jnp.sum(p, axis=1, keepdims=True),
                        alpha * acc_old + _dot(p.astype(BF16), vb)))
        for gi in range(ng):
            m_sc[gi], l_sc[gi], acc_sc[gi] = new[gi]

    jd = (qi * tq) // tk

    def full_body(j, carry):
        block(j, False)
        return carry

    lax.fori_loop(0, jd, full_body, 0)
    block(jd, True)
    for gi in range(ng):
        acc = acc_sc[gi] / l_sc[gi]
        out = jnp.zeros((tq, gw), F32)
        for i in range(g):
            out = jnp.where(lane == i, acc[i * tq:(i + 1) * tq], out)
        o_ref[0, :, gi * gw:(gi + 1) * gw] = out.astype(BF16)


def _fox_prompt(q, k, v, fq, fk, *, g, hd, tq, tk):
    b, s, w = q.shape
    gw = g * hd
    ng = w // gw
    return pl.pallas_call(
        functools.partial(_fox_prompt_kernel, g=g, ng=ng, hd=hd, tq=tq, tk=tk),
        grid=(b, s // tq),
        in_specs=[pl.BlockSpec((1, tq, w), lambda bi, qi: (bi, qi, 0)),
                  pl.BlockSpec((1, s, w), lambda bi, qi: (bi, 0, 0)),
                  pl.BlockSpec((1, s, w), lambda bi, qi: (bi, 0, 0)),
                  pl.BlockSpec((1, ng, tq, LANES), lambda bi, qi: (bi, 0, qi, 0)),
                  pl.BlockSpec((1, ng, SUBLANES, s), lambda bi, qi: (bi, 0, 0, 0))],
        out_specs=pl.BlockSpec((1, tq, w), lambda bi, qi: (bi, qi, 0)),
        out_shape=jax.ShapeDtypeStruct((b, s, w), BF16),
        scratch_shapes=[pltpu.VMEM((ng, g * tq, 1), F32), pltpu.VMEM((ng, g * tq, 1), F32),
                        pltpu.VMEM((ng, g * tq, gw), F32)],
        compiler_params=_cparams("arbitrary", "arbitrary"),
        name="fox_prompt",
    )(q, k, v, fq, fk)


def _fox_sample_kernel(pt_ref, q_ref, kn_ref, vn_ref, bq_ref, bnew_ref, bpast_ref, ck_hbm, cv_hbm, o_ref,
                       kbuf, vbuf, sem, *, n_pages, page, n_tok, n_heads, hd):
    b = pl.program_id(0)
    nb = pl.num_programs(0)
    slot = b % 2

    def copies(bb, sl):
        out = []
        for j in range(n_pages):
            pg = pt_ref[bb, j]
            out.append(pltpu.make_async_copy(ck_hbm.at[pg], kbuf.at[sl, j], sem.at[sl, 0]))
            out.append(pltpu.make_async_copy(cv_hbm.at[pg], vbuf.at[sl, j], sem.at[sl, 1]))
        return out

    @pl.when(b == 0)
    def _():
        for c in copies(0, 0):
            c.start()

    @pl.when(b + 1 < nb)
    def _():
        for c in copies(b + 1, 1 - slot):
            c.start()

    for c in copies(b, slot):
        c.wait()

    w = n_heads * hd
    q = q_ref[0].astype(F32)
    sub = lax.broadcasted_iota(jnp.int32, (n_heads, w), 0)
    lane_head = lax.broadcasted_iota(jnp.int32, (n_heads, w), 1) // hd
    mine = sub == lane_head
    q_bd = jnp.concatenate(
        [jnp.where(mine, jnp.broadcast_to(q[l:l + 1, :], (n_heads, w)), 0.0) for l in range(n_tok)],
        axis=0).astype(BF16)
    past = n_pages * page
    kp = kbuf[slot].reshape(past, w).astype(BF16)
    vp = vbuf[slot].reshape(past, w).astype(BF16)
    bq = bq_ref[0]
    s_past = _dot_nt(q_bd, kp) + bq + jnp.concatenate([bpast_ref[0]] * n_tok, axis=0)
    kn = kn_ref[0]
    vn = vn_ref[0]
    s_new = _dot_nt(q_bd, kn) + bnew_ref[0]
    r_tok = lax.broadcasted_iota(jnp.int32, s_new.shape, 0) // n_heads
    c_tok = lax.broadcasted_iota(jnp.int32, s_new.shape, 1)
    s_new = jnp.where(c_tok <= r_tok, s_new, -jnp.inf)
    m = jnp.maximum(jnp.max(s_past, axis=1, keepdims=True), jnp.max(s_new, axis=1, keepdims=True))
    p_past = jnp.exp2(s_past - m)
    p_new = jnp.exp2(s_new - m)
    denom = jnp.sum(p_past, axis=1, keepdims=True) + jnp.sum(p_new, axis=1, keepdims=True)
    o_all = (_dot(p_past.astype(BF16), vp) + _dot(p_new.astype(BF16), vn)) / denom
    rows = [jnp.sum(jnp.where(mine, o_all[l * n_heads:(l + 1) * n_heads], 0.0), axis=0, keepdims=True)
            for l in range(n_tok)]
    o_ref[0] = jnp.concatenate(rows, axis=0).astype(BF16)


def _fox_sample(page_table, q, kn, vn, bq, bnew, bpast, cache_k, cache_v, *, n_heads, hd):
    ns, n_tok, w = q.shape
    n_pages = page_table.shape[1]
    page = cache_k.shape[1]
    past = n_pages * page
    lh = n_tok * n_heads
    grid_spec = pltpu.PrefetchScalarGridSpec(
        num_scalar_prefetch=1,
        grid=(ns,),
        in_specs=[pl.BlockSpec((1, n_tok, w), lambda b, pt: (b, 0, 0)),
                  pl.BlockSpec((1, LANES, w), lambda b, pt: (b, 0, 0)),
                  pl.BlockSpec((1, LANES, w), lambda b, pt: (b, 0, 0)),
                  pl.BlockSpec((1, lh, 1), lambda b, pt: (b, 0, 0)),
                  pl.BlockSpec((1, lh, LANES), lambda b, pt: (b, 0, 0)),
                  pl.BlockSpec((1, n_heads, past), lambda b, pt: (b, 0, 0)),
                  pl.BlockSpec(memory_space=pl.ANY),
                  pl.BlockSpec(memory_space=pl.ANY)],
        out_specs=pl.BlockSpec((1, n_tok, w), lambda b, pt: (b, 0, 0)),
        scratch_shapes=[pltpu.VMEM((2, n_pages, page, w), F32), pltpu.VMEM((2, n_pages, page, w), F32),
                        pltpu.SemaphoreType.DMA((2, 2))],
    )
    return pl.pallas_call(
        functools.partial(_fox_sample_kernel, n_pages=n_pages, page=page, n_tok=n_tok, n_heads=n_heads,
                          hd=hd),
        grid_spec=grid_spec,
        out_shape=jax.ShapeDtypeStruct((ns, n_tok, w), BF16),
        compiler_params=_cparams("arbitrary"),
        name="fox_sample",
    )(page_table, q, kn, vn, bq, bnew, bpast, cache_k, cache_v)


def _post_kernel(oa_ref, ob_ref, gates_ref, x_ref, gt_ref, sc_ref, sh_ref, wba_ref, wbb_ref, wout_ref, g_ref,
                 b_ref, x1_ref, h2_ref, *, alpha, per_row):
    gt = gt_ref[...] if per_row else gt_ref[0]
    sc = sc_ref[...] if per_row else sc_ref[0]
    sh = sh_ref[...] if per_row else sh_ref[0]
    ya = _dot(oa_ref[...], wba_ref[...])
    yb = _dot(ob_ref[...], wbb_ref[...])
    d = ya.shape[1]
    gts = gates_ref[...]
    merged = jax.nn.sigmoid(gts[:, :d]) * ya + jax.nn.sigmoid(gts[:, d:]) * yb
    mo = _dot(merged.astype(BF16), wout_ref[...])
    x1 = _layernorm(alpha * x_ref[...] + gt * mo, g_ref[...], b_ref[...])
    x1_ref[...] = x1
    h2_ref[...] = (x1 * (1.0 + sc) + sh).astype(BF16)


def _post(oa, ob, gates, x, gt, sc, sh, wba, wbb, wout, g, b, *, alpha, seq_len, tm):
    t, d = x.shape
    width = oa.shape[1]
    per_row = gt.ndim == 2
    if per_row:
        mod_spec = pl.BlockSpec((tm, d), lambda i: (i, 0))
    else:
        steps_per_seq = seq_len // tm
        mod_spec = pl.BlockSpec((1, 1, d), lambda i: (i // steps_per_seq, 0, 0))
    row = lambda w_: pl.BlockSpec((tm, w_), lambda i: (i, 0))
    const = lambda r, c: pl.BlockSpec((r, c), lambda i: (0, 0), pipeline_mode=pl.Buffered(1))
    return pl.pallas_call(
        functools.partial(_post_kernel, alpha=alpha, per_row=per_row),
        grid=(t // tm,),
        in_specs=[row(width), row(width), row(2 * d), row(d), mod_spec, mod_spec, mod_spec,
                  const(width, d), const(width, d), const(d, d), const(1, d), const(1, d)],
        out_specs=[row(d), row(d)],
        out_shape=[jax.ShapeDtypeStruct((t, d), F32), jax.ShapeDtypeStruct((t, d), BF16)],
        compiler_params=_cparams("arbitrary"),
        name="post_mixer",
    )(oa, ob, gates, x, gt, sc, sh, wba, wbb, wout, g.reshape(1, d), b.reshape(1, d))


def _router_kernel(h_ref, wrt_ref, bias_ref, gates_ref, *, n_groups, topk_groups, top_k, scale):
    logits = _dot_nt(wrt_ref[...], h_ref[...])
    s = jax.nn.sigmoid(logits)
    sb = s + bias_ref[...]
    n_e, tm = s.shape
    gs = n_e // n_groups
    eidx = lax.broadcasted_iota(jnp.int32, (n_e, tm), 0)
    sub = lax.broadcasted_iota(jnp.int32, (gs, tm), 0)
    neg = -jnp.inf
    gscores = []
    for gi in range(n_groups):
        slab = sb[gi * gs:(gi + 1) * gs]
        m1 = jnp.max(slab, axis=0, keepdims=True)
        first = jnp.min(jnp.where(slab == m1, sub, gs), axis=0, keepdims=True)
        m2 = jnp.max(jnp.where(sub == first, neg, slab), axis=0, keepdims=True)
        gscores.append(m1 + m2)
    cur = jnp.concatenate(gscores, axis=0)
    gidx = lax.broadcasted_iota(jnp.int32, cur.shape, 0)
    gsel = jnp.zeros(cur.shape, jnp.bool_)
    for _ in range(topk_groups):
        mx = jnp.max(cur, axis=0, keepdims=True)
        first = jnp.min(jnp.where(cur == mx, gidx, n_groups), axis=0, keepdims=True)
        hit = gidx == first
        gsel = jnp.logical_or(gsel, hit)
        cur = jnp.where(hit, neg, cur)
    emask = jnp.concatenate([jnp.broadcast_to(gsel[gi:gi + 1], (gs, tm)) for gi in range(n_groups)], axis=0)
    cur = jnp.where(emask, sb, neg)
    sel = jnp.zeros(cur.shape, jnp.bool_)
    for _ in range(top_k):
        mx = jnp.max(cur, axis=0, keepdims=True)
        first = jnp.min(jnp.where(cur == mx, eidx, n_e), axis=0, keepdims=True)
        hit = eidx == first
        sel = jnp.logical_or(sel, hit)
        cur = jnp.where(hit, neg, cur)
    wsel = jnp.where(sel, s, 0.0)
    gates_ref[...] = wsel / jnp.sum(wsel, axis=0, keepdims=True) * scale


def _router(h2, wrt, bias, *, tm):
    t, d = h2.shape
    n_e = wrt.shape[0]
    return pl.pallas_call(
        functools.partial(_router_kernel, n_groups=N_GROUPS, topk_groups=TOPK_GROUPS, top_k=TOP_K,
                          scale=ROUTED_SCALE),
        grid=(t // tm,),
        in_specs=[pl.BlockSpec((tm, d), lambda i: (i, 0)),
                  pl.BlockSpec((n_e, d), lambda i: (0, 0)),
                  pl.BlockSpec((n_e, 1), lambda i: (0, 0))],
        out_specs=pl.BlockSpec((n_e, tm), lambda i: (0, i)),
        out_shape=jax.ShapeDtypeStruct((n_e, t), F32),
        compiler_params=_cparams("arbitrary"),
        name="router",
    )(h2, wrt, bias.reshape(n_e, 1))


def _moe_kernel(h_ref, gates_ref, wgu_ref, wd_ref, x1_ref, gt_ref, g_ref, b_ref, y_ref, acc_ref, *, eb, ff,
                alpha, per_row):
    e = pl.program_id(1)

    @pl.when(e == 0)
    def _():
        acc_ref[...] = jnp.zeros(acc_ref.shape, F32)

    h = h_ref[...]
    gts = gates_ref[...]
    lane = lax.broadcasted_iota(jnp.int32, gts.shape, 1)
    hs = []
    for i in range(eb):
        gu = _dot(h, wgu_ref[i])
        gate = gu[:, :ff]
        act = gate * jax.nn.sigmoid(gate) * gu[:, ff:]
        gcol = jnp.sum(jnp.where(lane == e * eb + i, gts, 0.0), axis=1, keepdims=True)
        hs.append((act * gcol).astype(BF16))
    hcat = jnp.concatenate(hs, axis=1)
    d = acc_ref.shape[1]
    acc_ref[...] += _dot(hcat, wd_ref[...].reshape(eb * ff, d))

    @pl.when(e == pl.num_programs(1) - 1)
    def _():
        gt = gt_ref[...] if per_row else gt_ref[0]
        y_ref[...] = _layernorm(alpha * x1_ref[...] + gt * acc_ref[...], g_ref[...], b_ref[...])


def _moe(h2, gates, wgu, wd, x1, gt, g, b, *, alpha, seq_len, tm, eb):
    t, d = h2.shape
    n_e, _, ff2 = wgu.shape
    ff = ff2 // 2
    per_row = gt.ndim == 2
    if per_row:
        mod_spec = pl.BlockSpec((tm, d), lambda i, e: (i, 0))
    else:
        steps_per_seq = seq_len // tm
        mod_spec = pl.BlockSpec((1, 1, d), lambda i, e: (i // steps_per_seq, 0, 0))
    return pl.pallas_call(
        functools.partial(_moe_kernel, eb=eb, ff=ff, alpha=alpha, per_row=per_row),
        grid=(t // tm, n_e // eb),
        in_specs=[pl.BlockSpec((tm, d), lambda i, e: (i, 0)),
                  pl.BlockSpec((tm, LANES), lambda i, e: (i, 0)),
                  pl.BlockSpec((eb, d, ff2), lambda i, e: (e, 0, 0)),
                  pl.BlockSpec((eb, ff, d), lambda i, e: (e, 0, 0)),
                  pl.BlockSpec((tm, d), lambda i, e: (i, 0)),
                  mod_spec,
                  pl.BlockSpec((1, d), lambda i, e: (0, 0)),
                  pl.BlockSpec((1, d), lambda i, e: (0, 0))],
        out_specs=pl.BlockSpec((tm, d), lambda i, e: (i, 0)),
        out_shape=jax.ShapeDtypeStruct((t, d), F32),
        scratch_shapes=[pltpu.VMEM((tm, d), F32)],
        compiler_params=_cparams("arbitrary", "arbitrary"),
        name="moe",
    )(h2, gates, wgu, wd, x1, gt, g.reshape(1, d), b.reshape(1, d))


def _block_ones(n, blk):
    i = jnp.arange(n) // blk
    return (i[:, None] == i[None, :]).astype(BF16)


def _expand_select(n_heads, hd):
    ch = jnp.arange(LANES)[:, None]
    col = jnp.arange(2 * n_heads * hd)[None, :]
    return (ch == col // hd).astype(BF16)


def _prep_weights(w_in, conv_dim, width_a, heads_a, width_b, heads_b, d):
    sizes = (conv_dim, width_a, heads_a, heads_a, width_b, width_b, width_b, heads_b, d, d)
    cuts, acc = [], 0
    for s in sizes[:-1]:
        acc += s
        cuts.append(acc)
    qkv_a, z_a, b_a, a_a, q_b, k_b, v_b, f_b, g_a, g_b = jnp.split(w_in, cuts, axis=-1)
    pad = jnp.zeros((d, LANES - 2 * heads_a - heads_b), w_in.dtype)
    return jnp.concatenate([g_a, g_b, qkv_a, z_a, q_b, k_b, v_b, b_a, a_a, f_b, pad], axis=1).astype(BF16)


def _mixer_and_ffn(x, mods, wts, *, seq_len, hist, ssm0, fox_fn, tm_proj, tm_moe, seq_rows, conv_rows,
                   gdn_pad_rows, heads_a, heads_b, hd_b, alpha):
    t, d = x.shape
    nb = t // seq_len
    sh1, sc1, gt1, sh2, sc2, gt2 = mods
    conv_dim = wts["conv_w"].shape[-1]
    width_a = conv_dim // 3
    hd_a = width_a // heads_a
    width_b = heads_b * hd_b
    gates, qkv_a, z_a, qb, kb, vb, kf, vf, small = _inproj(
        x, sc1, sh1, wts["w_in"], seq_len=seq_len, conv_dim=conv_dim, width=width_a, q_scale=hd_b ** -0.5 * LOG2E,
        tm=tm_proj)
    beta = jax.nn.sigmoid(small[:, :heads_a])
    g = -jnp.exp(wts["a_log"]) * jax.nn.softplus(small[:, heads_a:2 * heads_a] + wts["dt_bias"])
    logf = jax.nn.log_sigmoid(small[:, 2 * heads_a:2 * heads_a + heads_b] + wts["f_bias"])

    xa = qkv_a.reshape(nb, seq_len, conv_dim)
    conv_new = jnp.concatenate([hist[:, SUBLANES - (CONV_WIDTH - 1):], xa], axis=1)[:, seq_len:]
    if seq_rows == seq_len:
        first = 0
        qa, ka, va = _conv(xa.reshape(t, conv_dim), hist, wts["conv_w"], wts["bsum"], seq_rows=seq_len,
                           rows=conv_rows, width=width_a, q_scale=hd_a ** -0.5)
    else:
        first = SUBLANES
        stream = jnp.concatenate([hist, xa, jnp.zeros((nb, seq_rows - SUBLANES - seq_len, conv_dim), F32)], axis=1)
        qa, ka, va = _conv(stream.reshape(nb * seq_rows, conv_dim), jnp.zeros((1, SUBLANES, conv_dim), F32),
                           wts["conv_w"], wts["bsum"], seq_rows=nb * seq_rows, rows=conv_rows, width=width_a,
                           q_scale=hd_a ** -0.5)

    def seq_pad(a, rows_in, start):
        a = a.reshape(nb, rows_in, a.shape[-1])[:, start:start + seq_len]
        if gdn_pad_rows != seq_len:
            a = jnp.pad(a, ((0, 0), (0, gdn_pad_rows - seq_len), (0, 0)))
        return a.reshape(nb * gdn_pad_rows, a.shape[-1])

    qa, ka, va = (seq_pad(a, seq_rows, first) for a in (qa, ka, va))
    zp = seq_pad(z_a, seq_len, 0)
    beta_p = seq_pad(beta, seq_len, 0)
    g_p = seq_pad(g, seq_len, 0)
    n_ct = nb * gdn_pad_rows // GDN_CHUNK
    gc = jnp.cumsum(g_p.reshape(n_ct, GDN_CHUNK, heads_a), axis=1)
    aux = jnp.concatenate([beta_p, gc.reshape(-1, heads_a),
                           jnp.zeros((nb * gdn_pad_rows, LANES - 2 * heads_a), F32)], axis=1)
    grow = gc.reshape(n_ct, GDN_CHUNK, heads_a // 2, 2).transpose(0, 2, 3, 1).reshape(n_ct, heads_a // 2, LANES)
    w1, w2, qg, kd, aqk, gl = _gdn_prep(qa, ka, va, aux, grow, wts["esel"], chunk=GDN_CHUNK, n_chunks=4,
                                        n_live=min(seq_len, GDN_CHUNK))
    oa, ssm_new = _gdn_state(w1, w2, qg, kd, aqk, gl, zp, wts["norm_g"], ssm0, wts["bones"], chunk=GDN_CHUNK,
                             n_chunks=min(2, gdn_pad_rows // GDN_CHUNK), n_seq=min(4, nb),
                             seq_rows=gdn_pad_rows, hd=hd_a)
    if gdn_pad_rows != seq_len:
        oa = oa.reshape(nb, gdn_pad_rows, width_a)[:, :seq_len].reshape(t, width_a)

    ob = fox_fn(qb, kb, vb, logf)

    x1, h2 = _post(oa, ob, gates, x, gt1, sc2, sh2, wts["w_ba"], wts["w_bb"], wts["w_out"], wts["ln1_g"],
                   wts["ln1_b"], alpha=alpha, seq_len=seq_len, tm=tm_proj)
    gates_t = _router(h2, wts["w_router_t"], wts["router_bias"], tm=min(512, t))
    n_e = gates_t.shape[0]
    gates_m = jnp.concatenate([gates_t.T, jnp.ones((t, 1), F32), jnp.zeros((t, LANES - n_e - 1), F32)], axis=1)
    y = _moe(h2, gates_m, wts["w_gu"], wts["w_dn"], x1, gt2, wts["ln2_g"], wts["ln2_b"], alpha=alpha,
             seq_len=seq_len, tm=tm_moe, eb=5)
    return y, kf, vf, logf, ssm_new, conv_new


def kernel(x_prompt, x_sample, c_prompt, c_sample, cache_k, cache_v, cache_logf, state_ssm, state_conv, page_table, w_ada, b_ada, w_in, conv_w, gdn_a_log, gdn_dt_bias, gdn_norm_g, fox_f_bias, w_branch_a, w_branch_b, w_out, ln1_g, ln1_b, w_router, router_bias, w_gate, w_up, w_down, ws_gate, ws_up, ws_down, ln2_g, ln2_b):
    depth = w_ada.shape[0]
    bp, seq, d = x_prompt.shape
    ns, n_tok, _ = x_sample.shape
    n_pages = page_table.shape[1]
    page = cache_k.shape[2]
    heads_b, hd_b = cache_k.shape[3], cache_k.shape[4]
    width_b = heads_b * hd_b
    past = n_pages * page
    heads_a = gdn_a_log.shape[-1]
    conv_dim = conv_w.shape[-1]
    width_a = conv_dim // 3
    hd_a = width_a // heads_a
    assert hd_a * 2 == LANES and GDN_CHUNK == hd_a and seq % GDN_CHUNK == 0 and n_tok <= SUBLANES
    alpha = (2.0 * depth) ** 0.25
    g_heads = 4

    yp = x_prompt.reshape(bp * seq, d)
    ys = x_sample.reshape(ns * n_tok, d)
    outs = [[] for _ in range(10)]
    for l in range(depth):
        wts = dict(
            w_in=_prep_weights(w_in[l], conv_dim, width_a, heads_a, width_b, heads_b, d),
            conv_w=conv_w[l], a_log=gdn_a_log[l], dt_bias=gdn_dt_bias[l],
            norm_g=jnp.tile(gdn_norm_g[l], heads_a).reshape(1, width_a), f_bias=fox_f_bias[l],
            w_ba=w_branch_a[l].astype(BF16), w_bb=w_branch_b[l].astype(BF16), w_out=w_out[l].astype(BF16),
            ln1_g=ln1_g[l], ln1_b=ln1_b[l], w_router_t=w_router[l].T.astype(BF16), router_bias=router_bias[l],
            w_gu=jnp.concatenate([jnp.concatenate([w_gate[l], w_up[l]], axis=-1),
                                  jnp.concatenate([ws_gate[l], ws_up[l]], axis=-1)[None]], axis=0).astype(BF16),
            w_dn=jnp.concatenate([w_down[l], ws_down[l][None]], axis=0).astype(BF16),
            ln2_g=ln2_g[l], ln2_b=ln2_b[l],
            bsum=_block_ones(width_a, hd_a), bones=_block_ones(LANES, hd_a), esel=_expand_select(heads_a, hd_a))
        n_c = bp + ns
        c_all = jnp.concatenate([c_prompt, c_sample, jnp.zeros((-n_c % SUBLANES, d), F32)], axis=0)
        mod = _ada(c_all, w_ada[l], b_ada[l])
        mods_p = [m.reshape(bp, 1, d) for m in jnp.split(mod[:bp], 6, axis=-1)]
        mods_s = [jnp.repeat(m, n_tok, axis=0) for m in jnp.split(mod[bp:n_c], 6, axis=-1)]

        def fox_prompt(qb, kb, vb, logf):
            big_f = jnp.cumsum(logf.reshape(bp, seq, heads_b), axis=1) * LOG2E
            fg = big_f.reshape(bp, seq, heads_b // g_heads, g_heads)
            fq = jnp.pad(fg.transpose(0, 2, 1, 3), ((0, 0), (0, 0), (0, 0), (0, LANES - g_heads)))
            fk = jnp.pad(fg.transpose(0, 2, 3, 1), ((0, 0), (0, 0), (0, SUBLANES - g_heads), (0, 0)))
            o = _fox_prompt(qb.reshape(bp, seq, width_b), kb.reshape(bp, seq, width_b),
                            vb.reshape(bp, seq, width_b), fq, fk, g=g_heads, hd=hd_b, tq=128, tk=min(1024, seq))
            return o.reshape(bp * seq, width_b)

        yp, kp, vp, fp, sp, cp = _mixer_and_ffn(
            yp, mods_p, wts, seq_len=seq, hist=jnp.zeros((bp, SUBLANES, conv_dim), F32),
            ssm0=jnp.zeros((bp, heads_a, hd_a, hd_a), F32), fox_fn=fox_prompt, tm_proj=256, tm_moe=min(1024, seq),
            seq_rows=seq, conv_rows=256, gdn_pad_rows=seq, heads_a=heads_a, heads_b=heads_b, hd_b=hd_b,
            alpha=alpha)

        def fox_sample(qb, kb, vb, logf):
            fn = jnp.cumsum(logf.reshape(ns, n_tok, heads_b), axis=1) * LOG2E
            bq = fn.reshape(ns, n_tok * heads_b, 1)
            bnew = (fn[:, :, None, :] - fn[:, None, :, :]).transpose(0, 1, 3, 2).reshape(ns, n_tok * heads_b, n_tok)
            bnew = jnp.pad(bnew, ((0, 0), (0, 0), (0, LANES - n_tok)))
            lfp = jnp.take(cache_logf[l], page_table, axis=0).reshape(ns, past, heads_b)
            suffix = (lax.cumsum(lfp, axis=1, reverse=True) - lfp) * LOG2E
            bpast = suffix.transpose(0, 2, 1)
            pad_rows = lambda a: jnp.pad(a.reshape(ns, n_tok, width_b), ((0, 0), (0, LANES - n_tok), (0, 0)))
            o = _fox_sample(page_table, qb.reshape(ns, n_tok, width_b), pad_rows(kb), pad_rows(vb), bq, bnew,
                            bpast, cache_k[l].reshape(-1, page, width_b), cache_v[l].reshape(-1, page, width_b),
                            n_heads=heads_b, hd=hd_b)
            return o.reshape(ns * n_tok, width_b)

        hist_s = jnp.concatenate([jnp.zeros((ns, SUBLANES - (CONV_WIDTH - 1), conv_dim), F32), state_conv[l]],
                                 axis=1)
        ys, ks, vs, fs, ss, cs = _mixer_and_ffn(
            ys, mods_s, wts, seq_len=n_tok, hist=hist_s,
            ssm0=state_ssm[l], fox_fn=fox_sample, tm_proj=256, tm_moe=ns * n_tok, seq_rows=2 * SUBLANES,
            conv_rows=min(256, ns * 2 * SUBLANES), gdn_pad_rows=GDN_CHUNK, heads_a=heads_a, heads_b=heads_b, hd_b=hd_b,
            alpha=alpha)

        new = (kp.reshape(bp, seq, heads_b, hd_b), vp.reshape(bp, seq, heads_b, hd_b),
               fp.reshape(bp, seq, heads_b), sp, cp,
               ks.reshape(ns, n_tok, heads_b, hd_b), vs.reshape(ns, n_tok, heads_b, hd_b),
               fs.reshape(ns, n_tok, heads_b), ss, cs)
        for lst, a in zip(outs, new):
            lst.append(a)
    return (yp.reshape(bp, seq, d), ys.reshape(ns, n_tok, d)) + tuple(jnp.stack(o) for o in outs)
```

```python
import functools

import jax
import jax.numpy as jnp
from jax import lax
from jax.experimental import pallas as pl
from jax.experimental.pallas import tpu as pltpu

F32 = jnp.float32
BF16 = jnp.bfloat16

V7X_VMEM_BYTES = 64 * 1024 * 1024
VMEM_LIMIT_BYTES = V7X_VMEM_BYTES - 8 * 1024 * 1024
LANES = 128
SUBLANES = 8

N_GROUPS = 8
TOPK_GROUPS = 4
TOP_K = 8
ROUTED_SCALE = 2.5
CONV_WIDTH = 4
GDN_CHUNK = 64
LN_EPS = 1e-5
RMS_EPS = 1e-6
LOG2E = 1.4426950408889634


def _cparams(*sem):
    return pltpu.CompilerParams(dimension_semantics=sem, vmem_limit_bytes=VMEM_LIMIT_BYTES)


def _dot(a, b):
    return jnp.dot(a, b, preferred_element_type=F32)


def _dot_nt(a, b):
    return lax.dot_general(a, b, (((1,), (1,)), ((), ())), preferred_element_type=F32)


def _dot_tn(a, b):
    return lax.dot_general(a, b, (((0,), (0,)), ((), ())), preferred_element_type=F32)


def _split2(x):
    hi = x.astype(BF16)
    lo = (x - hi.astype(F32)).astype(BF16)
    return hi, lo


def _split3(x):
    hi = x.astype(BF16)
    r = x - hi.astype(F32)
    mid = r.astype(BF16)
    lo = (r - mid.astype(F32)).astype(BF16)
    return hi, mid, lo


def _dot_sel(x, sel, parts):
    r = x.shape[0]
    pieces = _split3(x) if parts == 3 else _split2(x)
    y = _dot(jnp.concatenate(pieces, axis=0), sel)
    out = y[:r]
    for i in range(1, parts):
        out = out + y[i * r:(i + 1) * r]
    return out


def _block_diag(x, half):
    lane = lax.broadcasted_iota(jnp.int32, x.shape, 1)
    left = lane < half
    zero = jnp.zeros_like(x)
    return jnp.concatenate([jnp.where(left, x, zero), jnp.where(left, zero, x)], axis=0)


def _layernorm(r, g, b):
    mu = jnp.mean(r, axis=-1, keepdims=True)
    rc = r - mu
    var = jnp.mean(rc * rc, axis=-1, keepdims=True)
    return rc * lax.rsqrt(var + LN_EPS) * g + b


def _ada_kernel(c_ref, w_ref, b_ref, o_ref):
    o_ref[...] = _dot(c_ref[...].astype(BF16), w_ref[...].astype(BF16)) + b_ref[...]


def _ada(c, w, b, tn=512):
    m, d = c.shape
    n = w.shape[1]
    return pl.pallas_call(
        _ada_kernel,
        grid=(n // tn,),
        in_specs=[pl.BlockSpec((m, d), lambda j: (0, 0)),
                  pl.BlockSpec((d, tn), lambda j: (0, j)),
                  pl.BlockSpec((1, tn), lambda j: (0, j))],
        out_specs=pl.BlockSpec((m, tn), lambda j: (0, j)),
        out_shape=jax.ShapeDtypeStruct((m, n), F32),
        compiler_params=_cparams("arbitrary"),
        name="ada",
    )(c, w, b.reshape(1, n))


def _inproj_kernel(x_ref, sc_ref, sh_ref, w_ref, gates_ref, qkv_ref, z_ref, qb_ref, kb_ref, vb_ref,
                   kf_ref, vf_ref, small_ref, *, d, conv_dim, width, q_scale, per_row):
    sc = sc_ref[...] if per_row else sc_ref[0]
    sh = sh_ref[...] if per_row else sh_ref[0]
    h = (x_ref[...] * (1.0 + sc) + sh).astype(BF16)
    off = [0]

    def mm(n):
        a = off[0]
        off[0] = a + n
        return _dot(h, w_ref[:, a:a + n])

    gates_ref[...] = mm(2 * d)
    qkv_ref[...] = mm(conv_dim)
    z_ref[...] = mm(width)
    qb_ref[...] = (mm(width) * q_scale).astype(BF16)
    k = mm(width)
    kf_ref[...] = k
    kb_ref[...] = k.astype(BF16)
    v = mm(width)
    vf_ref[...] = v
    vb_ref[...] = v.astype(BF16)
    small_ref[...] = mm(LANES)


def _inproj(x, sc, sh, w_perm, *, seq_len, conv_dim, width, q_scale, tm):
    t, d = x.shape
    n = w_perm.shape[1]
    per_row = sc.ndim == 2
    if per_row:
        mod_spec = pl.BlockSpec((tm, d), lambda i: (i, 0))
    else:
        steps_per_seq = seq_len // tm
        mod_spec = pl.BlockSpec((1, 1, d), lambda i: (i // steps_per_seq, 0, 0))
    row = lambda w_: pl.BlockSpec((tm, w_), lambda i: (i, 0))
    outs = [(2 * d, F32), (conv_dim, F32), (width, F32), (width, BF16), (width, BF16), (width, BF16),
            (width, F32), (width, F32), (LANES, F32)]
    return pl.pallas_call(
        functools.partial(_inproj_kernel, d=d, conv_dim=conv_dim, width=width, q_scale=q_scale,
                          per_row=per_row),
        grid=(t // tm,),
        in_specs=[row(d), mod_spec, mod_spec,
                  pl.BlockSpec((d, n), lambda i: (0, 0), pipeline_mode=pl.Buffered(1))],
        out_specs=[row(w_) for w_, _ in outs],
        out_shape=[jax.ShapeDtypeStruct((t, w_), dt) for w_, dt in outs],
        compiler_params=_cparams("arbitrary"),
        name="inproj",
    )(x, sc, sh, w_perm)


def _conv_kernel(x_ref, prev_ref, hist_ref, cw_ref, bsum_ref, q_ref, k_ref, v_ref, xbuf, *, rows, width,
                 q_scale):
    t = pl.program_id(1)
    x = x_ref[...]
    halo = jnp.where(t == 0, hist_ref[0], prev_ref[...])
    xbuf[0:SUBLANES, :] = halo
    xbuf[SUBLANES:SUBLANES + rows, :] = x
    cw = cw_ref[...]
    conv = x * cw[3:4]
    for i in range(CONV_WIDTH - 1):
        shift = CONV_WIDTH - 1 - i
        conv = conv + xbuf[pl.ds(SUBLANES - shift, rows), :] * cw[i:i + 1]
    y = conv * jax.nn.sigmoid(conv)
    q = y[:, :width]
    k = y[:, width:2 * width]
    bsum = bsum_ref[...]
    q_ref[...] = q * lax.rsqrt(_dot_sel(q * q, bsum, 2) + RMS_EPS) * q_scale
    k_ref[...] = k * lax.rsqrt(_dot_sel(k * k, bsum, 2) + RMS_EPS)
    v_ref[...] = y[:, 2 * width:]


def _conv(x, hist, cw, bsum, *, seq_rows, rows, width, q_scale):
    t, cd = x.shape
    nb = t // seq_rows
    tiles = seq_rows // rows
    rb = rows // SUBLANES
    outs = [jax.ShapeDtypeStruct((t, width), F32)] * 3
    return pl.pallas_call(
        functools.partial(_conv_kernel, rows=rows, width=width, q_scale=q_scale),
        grid=(nb, tiles),
        in_specs=[pl.BlockSpec((rows, cd), lambda b, i: (b * tiles + i, 0)),
                  pl.BlockSpec((SUBLANES, cd), lambda b, i: (jnp.maximum((b * tiles + i) * rb - 1, 0), 0)),
                  pl.BlockSpec((1, SUBLANES, cd), lambda b, i: (b, 0, 0)),
                  pl.BlockSpec((CONV_WIDTH, cd), lambda b, i: (0, 0)),
                  pl.BlockSpec((width, width), lambda b, i: (0, 0))],
        out_specs=[pl.BlockSpec((rows, width), lambda b, i: (b * tiles + i, 0))] * 3,
        out_shape=outs,
        scratch_shapes=[pltpu.VMEM((rows + SUBLANES, cd), F32)],
        compiler_params=_cparams("arbitrary", "arbitrary"),
        name="gdn_conv",
    )(x, x, hist, cw, bsum)


def _sbs(a, b, half):
    return _dot(a.astype(BF16), _block_diag(b.astype(BF16), half))


def _gdn_prep_kernel(q_ref, k_ref, v_ref, aux_ref, grow_ref, esel_ref, w1_ref, w2_ref, qg_ref, kd_ref,
                     aqk_ref, gl_ref, beta_s, gc_s, *, chunk, n_chunks, width, n_live):
    half = LANES // 2
    ex = _dot_sel(aux_ref[...], esel_ref[...], 3)
    beta_s[...] = ex[:, :width]
    gc_s[...] = ex[:, width:]
    ii = lax.broadcasted_iota(jnp.int32, (chunk, LANES), 0)
    jj = lax.broadcasted_iota(jnp.int32, (chunk, LANES), 1) & (half - 1)
    incl = ii >= jj
    strict = ii > jj
    eye = (ii == jj).astype(F32)

    streams = [(c, p) for c in range(n_chunks) for p in range(width // LANES)]
    st = []
    for c, p in streams:
        rows = slice(c * chunk, (c + 1) * chunk)
        ls = slice(p * LANES, (p + 1) * LANES)
        kp, qp, vp = k_ref[rows, ls], q_ref[rows, ls], v_ref[rows, ls]
        bb, gb = beta_s[rows, ls], gc_s[rows, ls]
        dec = jnp.where(incl, jnp.exp(jnp.where(incl, gb - grow_ref[c:c + 1, p, :], 0.0)), 0.0)
        kb = kp.astype(BF16)
        kbd = _block_diag(kb, half)
        lm = jnp.where(strict, bb * _dot_nt(kb, kbd) * dec, 0.0)
        st.append(dict(kp=kp, qp=qp, vp=vp, bb=bb, gb=gb, dec=dec, lm=lm, qk=_dot_nt(qp.astype(BF16), kbd),
                       dinv=eye - jnp.where((ii ^ jj) == 1, lm, 0.0)))
    s = 2
    while s < min(chunk, n_live):
        sel = ((ii // (2 * s)) == (jj // (2 * s))) & ((ii & s) != 0) & ((jj & s) == 0)
        tmp = [_sbs(x["dinv"], jnp.where(sel, x["lm"], 0.0), half) for x in st]
        for x, t_ in zip(st, tmp):
            x["dinv"] = x["dinv"] - _sbs(t_, x["dinv"], half)
        s *= 2
    for (c, p), x in zip(streams, st):
        rows = slice(c * chunk, (c + 1) * chunk)
        ls = slice(p * LANES, (p + 1) * LANES)
        gb, kp, bb = x["gb"], x["kp"], x["bb"]
        gam = jnp.exp(gb)
        rhs = jnp.concatenate([_block_diag((bb * x["vp"]).astype(BF16), half),
                               _block_diag((bb * gam * kp).astype(BF16), half)], axis=1)
        w = _dot(x["dinv"].astype(BF16), rhs)
        glast = gb[chunk - 1:chunk, :]
        w1_ref[rows, ls] = w[:, :LANES]
        w2_ref[rows, ls] = w[:, LANES:].astype(BF16)
        qg_ref[rows, ls] = (x["qp"] * gam).astype(BF16)
        kd_ref[rows, ls] = (kp * jnp.exp(glast - gb)).astype(BF16)
        aqk_ref[rows, ls] = (x["qk"] * x["dec"]).astype(BF16)
        gl_ref[c, :, ls] = jnp.exp(glast)


def _gdn_prep(q, k, v, aux, grow, esel, *, chunk, n_chunks, n_live):
    t, width = q.shape
    rows = chunk * n_chunks
    nct = t // chunk
    row = lambda dt: (pl.BlockSpec((rows, width), lambda i: (i, 0)), jax.ShapeDtypeStruct((t, width), dt))
    outs = [row(F32), row(BF16), row(BF16), row(BF16), row(BF16),
            (pl.BlockSpec((n_chunks, 1, width), lambda i: (i, 0, 0)),
             jax.ShapeDtypeStruct((nct, 1, width), F32))]
    return pl.pallas_call(
        functools.partial(_gdn_prep_kernel, chunk=chunk, n_chunks=n_chunks, width=width, n_live=n_live),
        grid=(t // rows,),
        in_specs=[pl.BlockSpec((rows, width), lambda i: (i, 0))] * 3 + [
            pl.BlockSpec((rows, LANES), lambda i: (i, 0)),
            pl.BlockSpec((n_chunks, width // LANES, LANES), lambda i: (i, 0, 0)),
            pl.BlockSpec((LANES, 2 * width), lambda i: (0, 0))],
        out_specs=[o[0] for o in outs],
        out_shape=[o[1] for o in outs],
        scratch_shapes=[pltpu.VMEM((rows, width), F32), pltpu.VMEM((rows, width), F32)],
        compiler_params=_cparams("arbitrary"),
        name="gdn_prep",
    )(q, k, v, aux, grow, esel)


def _gdn_state_kernel(w1_ref, w2_ref, qg_ref, kd_ref, aqk_ref, gl_ref, z_ref, ng_ref, s0_ref, bones_ref,
                      o_ref, sout_ref, s_sc, *, chunk, n_chunks, n_seq, width, hd):
    half = LANES // 2
    n_pairs = width // LANES
    c = pl.program_id(1)
    streams = [(b, p) for b in range(n_seq) for p in range(n_pairs)]

    @pl.when(c == 0)
    def _():
        for i, (b, p) in enumerate(streams):
            pair = jnp.concatenate([s0_ref[b, 2 * p], s0_ref[b, 2 * p + 1]], axis=1)
            s_sc[i] = _block_diag(pair, half)

    r_i = lax.broadcasted_iota(jnp.int32, (LANES, LANES), 0)
    c_i = lax.broadcasted_iota(jnp.int32, (LANES, LANES), 1)
    same_head = (r_i < half) == (c_i < half)
    bones = bones_ref[...]
    for cc in range(n_chunks):
        rows = slice(cc * chunk, (cc + 1) * chunk)
        ins = []
        for i, (b, p) in enumerate(streams):
            ls = slice(p * LANES, (p + 1) * LANES)
            ins.append((s_sc[i], w1_ref[b, rows, ls], w2_ref[b, rows, ls], qg_ref[b, rows, ls],
                        kd_ref[b, rows, ls], aqk_ref[b, rows, ls], gl_ref[b, cc, :, ls], z_ref[b, rows, ls]))
        outs = []
        for (b, p), (s, w1, w2, qg, kd, aqk, gl, zz) in zip(streams, ins):
            ls = slice(p * LANES, (p + 1) * LANES)
            sb = s.astype(BF16)
            u = w1 - _dot(w2, sb)
            ub = u.astype(BF16)
            o = _dot(qg, sb) + _dot(aqk, _block_diag(ub, half))
            s_new = jnp.where(same_head, gl * s + _dot_tn(kd, ub), 0.0)
            ms = _dot_sel(o * o, bones, 2) * (1.0 / hd)
            on = o * lax.rsqrt(ms + RMS_EPS) * ng_ref[:, ls] * (zz * jax.nn.sigmoid(zz))
            outs.append((s_new, on.astype(BF16)))
        for i, (b, p) in enumerate(streams):
            s_sc[i] = outs[i][0]
            o_ref[b, rows, p * LANES:(p + 1) * LANES] = outs[i][1]

    @pl.when(c == pl.num_programs(1) - 1)
    def _():
        for i, (b, p) in enumerate(streams):
            s = s_sc[i]
            sout_ref[b, 2 * p] = s[:half, :half]
            sout_ref[b, 2 * p + 1] = s[half:, half:]


def _gdn_state(w1, w2, qg, kd, aqk, gl, z, ng, s0, bones, *, chunk, n_chunks, n_seq, seq_rows, hd):
    t, width = w1.shape
    nb = t // seq_rows
    rows = chunk * n_chunks
    nh = s0.shape[1]
    as_seq = lambda a: a.reshape(nb, seq_rows, width)
    row = pl.BlockSpec((n_seq, rows, width), lambda b, i: (b, i, 0))
    state = pl.BlockSpec((n_seq, nh, hd, hd), lambda b, i: (b, 0, 0, 0))
    o, s_out = pl.pallas_call(
        functools.partial(_gdn_state_kernel, chunk=chunk, n_chunks=n_chunks, n_seq=n_seq, width=width, hd=hd),
        grid=(nb // n_seq, seq_rows // rows),
        in_specs=[row, row, row, row, row,
                  pl.BlockSpec((n_seq, n_chunks, 1, width), lambda b, i: (b, i, 0, 0)),
                  row,
                  pl.BlockSpec((1, width), lambda b, i: (0, 0)),
                  state,
                  pl.BlockSpec((LANES, LANES), lambda b, i: (0, 0))],
        out_specs=[row, state],
        out_shape=[jax.ShapeDtypeStruct((nb, seq_rows, width), BF16), jax.ShapeDtypeStruct(s0.shape, F32)],
        scratch_shapes=[pltpu.VMEM((n_seq * (width // LANES), LANES, LANES), F32)],
        compiler_params=_cparams("arbitrary", "arbitrary"),
        name="gdn_state",
    )(as_seq(w1), as_seq(w2), as_seq(qg), as_seq(kd), as_seq(aqk),
      gl.reshape(nb, seq_rows // chunk, 1, width), as_seq(z), ng, s0, bones)
    return o.reshape(t, width), s_out


def _fox_prompt_kernel(q_ref, k_ref, v_ref, fq_ref, fk_ref, o_ref, m_sc, l_sc, acc_sc, *, g, ng, hd, tq, tk):
    qi = pl.program_id(1)
    gw = g * hd
    lane = lax.broadcasted_iota(jnp.int32, (tq, gw), 1) // hd
    q_bd, fq_col = [], []
    for gi in range(ng):
        q = q_ref[0, :, gi * gw:(gi + 1) * gw]
        zero = jnp.zeros_like(q)
        q_bd.append(jnp.concatenate([jnp.where(lane == i, q, zero) for i in range(g)], axis=0))
        fq = fq_ref[0, gi]
        fq_col.append(jnp.concatenate([fq[:, i:i + 1] for i in range(g)], axis=0))
    m_sc[...] = jnp.full(m_sc.shape, -jnp.inf, F32)
    l_sc[...] = jnp.zeros(l_sc.shape, F32)
    acc_sc[...] = jnp.zeros(acc_sc.shape, F32)

    def block(j, masked):
        ks = pl.ds(pl.multiple_of(j * tk, tk), tk)
        if masked:
            rr = lax.broadcasted_iota(jnp.int32, (tq, tk), 0) + qi * tq
            cc = lax.broadcasted_iota(jnp.int32, (tq, tk), 1) + j * tk
            ok = jnp.concatenate([rr >= cc] * g, axis=0)
        old = [(m_sc[gi], l_sc[gi], acc_sc[gi]) for gi in range(ng)]
        new = []
        for gi in range(ng):
            kb = k_ref[0, ks, gi * gw:(gi + 1) * gw]
            vb = v_ref[0, ks, gi * gw:(gi + 1) * gw]
            fk = fk_ref[0, gi, :, ks]
            fk_full = jnp.concatenate([jnp.broadcast_to(fk[i:i + 1, :], (tq, tk)) for i in range(g)], axis=0)
            s = _dot_nt(q_bd[gi], kb) + fq_col[gi] - fk_full
            if masked:
                s = jnp.where(ok, s, -jnp.inf)
            m_old, l_old, acc_old = old[gi]
            m_new = jnp.maximum(m_old, jnp.max(s, axis=1, keepdims=True))
            alpha = jnp.exp2(m_old - m_new)
            p = jnp.exp2(s - m_new)
            new.append((m_new, alpha * l_old + jnp.sum(p, axis=1, keepdims=True),
                        alpha * acc_old + _dot(p.astype(BF16), vb)))
        for gi in range(ng):
            m_sc[gi], l_sc[gi], acc_sc[gi] = new[gi]

    jd = (qi * tq) // tk

    def full_body(j, carry):
        block(j, False)
        return carry

    lax.fori_loop(0, jd, full_body, 0)
    block(jd, True)
    for gi in range(ng):
        acc = acc_sc[gi] / l_sc[gi]
        out = jnp.zeros((tq, gw), F32)
        for i in range(g):
            out = jnp.where(lane == i, acc[i * tq:(i + 1) * tq], out)
        o_ref[0, :, gi * gw:(gi + 1) * gw] = out.astype(BF16)


def _fox_prompt(q, k, v, fq, fk, *, g, hd, tq, tk):
    b, s, w = q.shape
    gw = g * hd
    ng = w // gw
    return pl.pallas_call(
        functools.partial(_fox_prompt_kernel, g=g, ng=ng, hd=hd, tq=tq, tk=tk),
        grid=(b, s // tq),
        in_specs=[pl.BlockSpec((1, tq, w), lambda bi, qi: (bi, qi, 0)),
                  pl.BlockSpec((1, s, w), lambda bi, qi: (bi, 0, 0)),
                  pl.BlockSpec((1, s, w), lambda bi, qi: (bi, 0, 0)),
                  pl.BlockSpec((1, ng, tq, LANES), lambda bi, qi: (bi, 0, qi, 0)),
                  pl.BlockSpec((1, ng, SUBLANES, s), lambda bi, qi: (bi, 0, 0, 0))],
        out_specs=pl.BlockSpec((1, tq, w), lambda bi, qi: (bi, qi, 0)),
        out_shape=jax.ShapeDtypeStruct((b, s, w), BF16),
        scratch_shapes=[pltpu.VMEM((ng, g * tq, 1), F32), pltpu.VMEM((ng, g * tq, 1), F32),
                        pltpu.VMEM((ng, g * tq, gw), F32)],
        compiler_params=_cparams("arbitrary", "arbitrary"),
        name="fox_prompt",
    )(q, k, v, fq, fk)


def _fox_sample_kernel(pt_ref, q_ref, kn_ref, vn_ref, bq_ref, bnew_ref, bpast_ref, ck_hbm, cv_hbm, o_ref,
                       kbuf, vbuf, sem, *, n_pages, page, n_tok, n_heads, hd):
    b = pl.program_id(0)
    nb = pl.num_programs(0)
    slot = b % 2

    def copies(bb, sl):
        out = []
        for j in range(n_pages):
            pg = pt_ref[bb, j]
            out.append(pltpu.make_async_copy(ck_hbm.at[pg], kbuf.at[sl, j], sem.at[sl, 0]))
            out.append(pltpu.make_async_copy(cv_hbm.at[pg], vbuf.at[sl, j], sem.at[sl, 1]))
        return out

    @pl.when(b == 0)
    def _():
        for c in copies(0, 0):
            c.start()

    @pl.when(b + 1 < nb)
    def _():
        for c in copies(b + 1, 1 - slot):
            c.start()

    for c in copies(b, slot):
        c.wait()

    w = n_heads * hd
    q = q_ref[0].astype(F32)
    sub = lax.broadcasted_iota(jnp.int32, (n_heads, w), 0)
    lane_head = lax.broadcasted_iota(jnp.int32, (n_heads, w), 1) // hd
    mine = sub == lane_head
    q_bd = jnp.concatenate(
        [jnp.where(mine, jnp.broadcast_to(q[l:l + 1, :], (n_heads, w)), 0.0) for l in range(n_tok)],
        axis=0).astype(BF16)
    past = n_pages * page
    kp = kbuf[slot].reshape(past, w).astype(BF16)
    vp = vbuf[slot].reshape(past, w).astype(BF16)
    bq = bq_ref[0]
    s_past = _dot_nt(q_bd, kp) + bq + jnp.concatenate([bpast_ref[0]] * n_tok, axis=0)
    kn = kn_ref[0]
    vn = vn_ref[0]
    s_new = _dot_nt(q_bd, kn) + bnew_ref[0]
    r_tok = lax.broadcasted_iota(jnp.int32, s_new.shape, 0) // n_heads
    c_tok = lax.broadcasted_iota(jnp.int32, s_new.shape, 1)
    s_new = jnp.where(c_tok <= r_tok, s_new, -jnp.inf)
    m = jnp.maximum(jnp.max(s_past, axis=1, keepdims=True), jnp.max(s_new, axis=1, keepdims=True))
    p_past = jnp.exp2(s_past - m)
    p_new = jnp.exp2(s_new - m)
    denom = jnp.sum(p_past, axis=1, keepdims=True) + jnp.sum(p_new, axis=1, keepdims=True)
    o_all = (_dot(p_past.astype(BF16), vp) + _dot(p_new.astype(BF16), vn)) / denom
    rows = [jnp.sum(jnp.where(mine, o_all[l * n_heads:(l + 1) * n_heads], 0.0), axis=0, keepdims=True)
            for l in range(n_tok)]
    o_ref[0] = jnp.concatenate(rows, axis=0).astype(BF16)


def _fox_sample(page_table, q, kn, vn, bq, bnew, bpast, cache_k, cache_v, *, n_heads, hd):
    ns, n_tok, w = q.shape
    n_pages = page_table.shape[1]
    page = cache_k.shape[1]
    past = n_pages * page
    lh = n_tok * n_heads
    grid_spec = pltpu.PrefetchScalarGridSpec(
        num_scalar_prefetch=1,
        grid=(ns,),
        in_specs=[pl.BlockSpec((1, n_tok, w), lambda b, pt: (b, 0, 0)),
                  pl.BlockSpec((1, LANES, w), lambda b, pt: (b, 0, 0)),
                  pl.BlockSpec((1, LANES, w), lambda b, pt: (b, 0, 0)),
                  pl.BlockSpec((1, lh, 1), lambda b, pt: (b, 0, 0)),
                  pl.BlockSpec((1, lh, LANES), lambda b, pt: (b, 0, 0)),
                  pl.BlockSpec((1, n_heads, past), lambda b, pt: (b, 0, 0)),
                  pl.BlockSpec(memory_space=pl.ANY),
                  pl.BlockSpec(memory_space=pl.ANY)],
        out_specs=pl.BlockSpec((1, n_tok, w), lambda b, pt: (b, 0, 0)),
        scratch_shapes=[pltpu.VMEM((2, n_pages, page, w), cache_k.dtype),
                        pltpu.VMEM((2, n_pages, page, w), cache_v.dtype),
                        pltpu.SemaphoreType.DMA((2, 2))],
    )
    return pl.pallas_call(
        functools.partial(_fox_sample_kernel, n_pages=n_pages, page=page, n_tok=n_tok, n_heads=n_heads,
                          hd=hd),
        grid_spec=grid_spec,
        out_shape=jax.ShapeDtypeStruct((ns, n_tok, w), BF16),
        compiler_params=_cparams("arbitrary"),
        name="fox_sample",
    )(page_table, q, kn, vn, bq, bnew, bpast, cache_k, cache_v)


def _post_kernel(oa_ref, ob_ref, gates_ref, x_ref, gt_ref, sc_ref, sh_ref, wba_ref, wbb_ref, wout_ref, g_ref,
                 b_ref, x1_ref, h2_ref, *, alpha, per_row):
    gt = gt_ref[...] if per_row else gt_ref[0]
    sc = sc_ref[...] if per_row else sc_ref[0]
    sh = sh_ref[...] if per_row else sh_ref[0]
    ya = _dot(oa_ref[...], wba_ref[...])
    yb = _dot(ob_ref[...], wbb_ref[...])
    d = ya.shape[1]
    gts = gates_ref[...]
    merged = jax.nn.sigmoid(gts[:, :d]) * ya + jax.nn.sigmoid(gts[:, d:]) * yb
    mo = _dot(merged.astype(BF16), wout_ref[...])
    x1 = _layernorm(alpha * x_ref[...] + gt * mo, g_ref[...], b_ref[...])
    x1_ref[...] = x1
    h2_ref[...] = (x1 * (1.0 + sc) + sh).astype(BF16)


def _post(oa, ob, gates, x, gt, sc, sh, wba, wbb, wout, g, b, *, alpha, seq_len, tm):
    t, d = x.shape
    width = oa.shape[1]
    per_row = gt.ndim == 2
    if per_row:
        mod_spec = pl.BlockSpec((tm, d), lambda i: (i, 0))
    else:
        steps_per_seq = seq_len // tm
        mod_spec = pl.BlockSpec((1, 1, d), lambda i: (i // steps_per_seq, 0, 0))
    row = lambda w_: pl.BlockSpec((tm, w_), lambda i: (i, 0))
    const = lambda r, c: pl.BlockSpec((r, c), lambda i: (0, 0), pipeline_mode=pl.Buffered(1))
    return pl.pallas_call(
        functools.partial(_post_kernel, alpha=alpha, per_row=per_row),
        grid=(t // tm,),
        in_specs=[row(width), row(width), row(2 * d), row(d), mod_spec, mod_spec, mod_spec,
                  const(width, d), const(width, d), const(d, d), const(1, d), const(1, d)],
        out_specs=[row(d), row(d)],
        out_shape=[jax.ShapeDtypeStruct((t, d), F32), jax.ShapeDtypeStruct((t, d), BF16)],
        compiler_params=_cparams("arbitrary"),
        name="post_mixer",
    )(oa, ob, gates, x, gt, sc, sh, wba, wbb, wout, g.reshape(1, d), b.reshape(1, d))


def _router_kernel(h_ref, wrt_ref, bias_ref, gates_ref, *, n_groups, topk_groups, top_k, scale):
    logits = _dot_nt(wrt_ref[...], h_ref[...])
    s = jax.nn.sigmoid(logits)
    sb = s + bias_ref[...]
    n_e, tm = s.shape
    gs = n_e // n_groups
    eidx = lax.broadcasted_iota(jnp.int32, (n_e, tm), 0)
    sub = lax.broadcasted_iota(jnp.int32, (gs, tm), 0)
    neg = -jnp.inf
    gscores = []
    for gi in range(n_groups):
        slab = sb[gi * gs:(gi + 1) * gs]
        m1 = jnp.max(slab, axis=0, keepdims=True)
        first = jnp.min(jnp.where(slab == m1, sub, gs), axis=0, keepdims=True)
        m2 = jnp.max(jnp.where(sub == first, neg, slab), axis=0, keepdims=True)
        gscores.append(m1 + m2)
    cur = jnp.concatenate(gscores, axis=0)
    gidx = lax.broadcasted_iota(jnp.int32, cur.shape, 0)
    gsel = jnp.zeros(cur.shape, jnp.bool_)
    for _ in range(topk_groups):
        mx = jnp.max(cur, axis=0, keepdims=True)
        first = jnp.min(jnp.where(cur == mx, gidx, n_groups), axis=0, keepdims=True)
        hit = gidx == first
        gsel = jnp.logical_or(gsel, hit)
        cur = jnp.where(hit, neg, cur)
    emask = jnp.concatenate([jnp.broadcast_to(gsel[gi:gi + 1], (gs, tm)) for gi in range(n_groups)], axis=0)
    cur = jnp.where(emask, sb, neg)
    sel = jnp.zeros(cur.shape, jnp.bool_)
    for _ in range(top_k):
        mx = jnp.max(cur, axis=0, keepdims=True)
        first = jnp.min(jnp.where(cur == mx, eidx, n_e), axis=0, keepdims=True)
        hit = eidx == first
        sel = jnp.logical_or(sel, hit)
        cur = jnp.where(hit, neg, cur)
    wsel = jnp.where(sel, s, 0.0)
    gates_ref[...] = wsel / jnp.sum(wsel, axis=0, keepdims=True) * scale


def _router(h2, wrt, bias, *, tm):
    t, d = h2.shape
    n_e = wrt.shape[0]
    return pl.pallas_call(
        functools.partial(_router_kernel, n_groups=N_GROUPS, topk_groups=TOPK_GROUPS, top_k=TOP_K,
                          scale=ROUTED_SCALE),
        grid=(t // tm,),
        in_specs=[pl.BlockSpec((tm, d), lambda i: (i, 0)),
                  pl.BlockSpec((n_e, d), lambda i: (0, 0)),
                  pl.BlockSpec((n_e, 1), lambda i: (0, 0))],
        out_specs=pl.BlockSpec((n_e, tm), lambda i: (0, i)),
        out_shape=jax.ShapeDtypeStruct((n_e, t), F32),
        compiler_params=_cparams("arbitrary"),
        name="router",
    )(h2, wrt, bias.reshape(n_e, 1))


def _moe_kernel(h_ref, gates_ref, wgu_ref, wd_ref, x1_ref, gt_ref, g_ref, b_ref, y_ref, acc_ref, *, eb, ff,
                alpha, per_row):
    e = pl.program_id(1)

    @pl.when(e == 0)
    def _():
        acc_ref[...] = jnp.zeros(acc_ref.shape, F32)

    h = h_ref[...]
    gts = gates_ref[...]
    lane = lax.broadcasted_iota(jnp.int32, gts.shape, 1)
    hs = []
    for i in range(eb):
        gu = _dot(h, wgu_ref[i])
        gate = gu[:, :ff]
        act = gate * jax.nn.sigmoid(gate) * gu[:, ff:]
        gcol = jnp.sum(jnp.where(lane == e * eb + i, gts, 0.0), axis=1, keepdims=True)
        hs.append((act * gcol).astype(BF16))
    hcat = jnp.concatenate(hs, axis=1)
    d = acc_ref.shape[1]
    acc_ref[...] += _dot(hcat, wd_ref[...].reshape(eb * ff, d))

    @pl.when(e == pl.num_programs(1) - 1)
    def _():
        gt = gt_ref[...] if per_row else gt_ref[0]
        y_ref[...] = _layernorm(alpha * x1_ref[...] + gt * acc_ref[...], g_ref[...], b_ref[...])


def _moe(h2, gates, wgu, wd, x1, gt, g, b, *, alpha, seq_len, tm, eb):
    t, d = h2.shape
    n_e, _, ff2 = wgu.shape
    ff = ff2 // 2
    per_row = gt.ndim == 2
    if per_row:
        mod_spec = pl.BlockSpec((tm, d), lambda i, e: (i, 0))
    else:
        steps_per_seq = seq_len // tm
        mod_spec = pl.BlockSpec((1, 1, d), lambda i, e: (i // steps_per_seq, 0, 0))
    return pl.pallas_call(
        functools.partial(_moe_kernel, eb=eb, ff=ff, alpha=alpha, per_row=per_row),
        grid=(t // tm, n_e // eb),
        in_specs=[pl.BlockSpec((tm, d), lambda i, e: (i, 0)),
                  pl.BlockSpec((tm, LANES), lambda i, e: (i, 0)),
                  pl.BlockSpec((eb, d, ff2), lambda i, e: (e, 0, 0)),
                  pl.BlockSpec((eb, ff, d), lambda i, e: (e, 0, 0)),
                  pl.BlockSpec((tm, d), lambda i, e: (i, 0)),
                  mod_spec,
                  pl.BlockSpec((1, d), lambda i, e: (0, 0)),
                  pl.BlockSpec((1, d), lambda i, e: (0, 0))],
        out_specs=pl.BlockSpec((tm, d), lambda i, e: (i, 0)),
        out_shape=jax.ShapeDtypeStruct((t, d), F32),
        scratch_shapes=[pltpu.VMEM((tm, d), F32)],
        compiler_params=_cparams("arbitrary", "arbitrary"),
        name="moe",
    )(h2, gates, wgu, wd, x1, gt, g.reshape(1, d), b.reshape(1, d))


def _block_ones(n, blk):
    i = jnp.arange(n) // blk
    return (i[:, None] == i[None, :]).astype(BF16)


def _expand_select(n_heads, hd):
    ch = jnp.arange(LANES)[:, None]
    col = jnp.arange(2 * n_heads * hd)[None, :]
    return (ch == col // hd).astype(BF16)


def _prep_weights(w_in, conv_dim, width_a, heads_a, width_b, heads_b, d):
    sizes = (conv_dim, width_a, heads_a, heads_a, width_b, width_b, width_b, heads_b, d, d)
    cuts, acc = [], 0
    for s in sizes[:-1]:
        acc += s
        cuts.append(acc)
    qkv_a, z_a, b_a, a_a, q_b, k_b, v_b, f_b, g_a, g_b = jnp.split(w_in, cuts, axis=-1)
    pad = jnp.zeros((d, LANES - 2 * heads_a - heads_b), w_in.dtype)
    return jnp.concatenate([g_a, g_b, qkv_a, z_a, q_b, k_b, v_b, b_a, a_a, f_b, pad], axis=1).astype(BF16)


def _mixer_and_ffn(x, mods, wts, *, seq_len, hist, ssm0, fox_fn, tm_proj, tm_moe, seq_rows, conv_rows,
                   gdn_pad_rows, heads_a, heads_b, hd_b, alpha):
    t, d = x.shape
    nb = t // seq_len
    sh1, sc1, gt1, sh2, sc2, gt2 = mods
    conv_dim = wts["conv_w"].shape[-1]
    width_a = conv_dim // 3
    hd_a = width_a // heads_a
    width_b = heads_b * hd_b
    gates, qkv_a, z_a, qb, kb, vb, kf, vf, small = _inproj(
        x, sc1, sh1, wts["w_in"], seq_len=seq_len, conv_dim=conv_dim, width=width_a, q_scale=hd_b ** -0.5 * LOG2E,
        tm=tm_proj)
    beta = jax.nn.sigmoid(small[:, :heads_a])
    g = -jnp.exp(wts["a_log"]) * jax.nn.softplus(small[:, heads_a:2 * heads_a] + wts["dt_bias"])
    logf = jax.nn.log_sigmoid(small[:, 2 * heads_a:2 * heads_a + heads_b] + wts["f_bias"])

    xa = qkv_a.reshape(nb, seq_len, conv_dim)
    conv_new = jnp.concatenate([hist[:, SUBLANES - (CONV_WIDTH - 1):], xa], axis=1)[:, seq_len:]
    if seq_rows == seq_len:
        first = 0
        qa, ka, va = _conv(xa.reshape(t, conv_dim), hist, wts["conv_w"], wts["bsum"], seq_rows=seq_len,
                           rows=conv_rows, width=width_a, q_scale=hd_a ** -0.5)
    else:
        first = SUBLANES
        stream = jnp.concatenate([hist, xa, jnp.zeros((nb, seq_rows - SUBLANES - seq_len, conv_dim), F32)], axis=1)
        qa, ka, va = _conv(stream.reshape(nb * seq_rows, conv_dim), jnp.zeros((1, SUBLANES, conv_dim), F32),
                           wts["conv_w"], wts["bsum"], seq_rows=nb * seq_rows, rows=conv_rows, width=width_a,
                           q_scale=hd_a ** -0.5)

    def seq_pad(a, rows_in, start):
        a = a.reshape(nb, rows_in, a.shape[-1])[:, start:start + seq_len]
        if gdn_pad_rows != seq_len:
            a = jnp.pad(a, ((0, 0), (0, gdn_pad_rows - seq_len), (0, 0)))
        return a.reshape(nb * gdn_pad_rows, a.shape[-1])

    qa, ka, va = (seq_pad(a, seq_rows, first) for a in (qa, ka, va))
    zp = seq_pad(z_a, seq_len, 0)
    beta_p = seq_pad(beta, seq_len, 0)
    g_p = seq_pad(g, seq_len, 0)
    n_ct = nb * gdn_pad_rows // GDN_CHUNK
    gc = jnp.cumsum(g_p.reshape(n_ct, GDN_CHUNK, heads_a), axis=1)
    aux = jnp.concatenate([beta_p, gc.reshape(-1, heads_a),
                           jnp.zeros((nb * gdn_pad_rows, LANES - 2 * heads_a), F32)], axis=1)
    grow = gc.reshape(n_ct, GDN_CHUNK, heads_a // 2, 2).transpose(0, 2, 3, 1).reshape(n_ct, heads_a // 2, LANES)
    w1, w2, qg, kd, aqk, gl = _gdn_prep(qa, ka, va, aux, grow, wts["esel"], chunk=GDN_CHUNK, n_chunks=4,
                                        n_live=min(seq_len, GDN_CHUNK))
    oa, ssm_new = _gdn_state(w1, w2, qg, kd, aqk, gl, zp, wts["norm_g"], ssm0, wts["bones"], chunk=GDN_CHUNK,
                             n_chunks=min(2, gdn_pad_rows // GDN_CHUNK), n_seq=min(4, nb),
                             seq_rows=gdn_pad_rows, hd=hd_a)
    if gdn_pad_rows != seq_len:
        oa = oa.reshape(nb, gdn_pad_rows, width_a)[:, :seq_len].reshape(t, width_a)

    ob = fox_fn(qb, kb, vb, logf)

    x1, h2 = _post(oa, ob, gates, x, gt1, sc2, sh2, wts["w_ba"], wts["w_bb"], wts["w_out"], wts["ln1_g"],
                   wts["ln1_b"], alpha=alpha, seq_len=seq_len, tm=tm_proj)
    gates_t = _router(h2, wts["w_router_t"], wts["router_bias"], tm=min(512, t))
    n_e = gates_t.shape[0]
    gates_m = jnp.concatenate([gates_t.T, jnp.ones((t, 1), F32), jnp.zeros((t, LANES - n_e - 1), F32)], axis=1)
    y = _moe(h2, gates_m, wts["w_gu"], wts["w_dn"], x1, gt2, wts["ln2_g"], wts["ln2_b"], alpha=alpha,
             seq_len=seq_len, tm=tm_moe, eb=5)
    return y, kf, vf, logf, ssm_new, conv_new


def kernel(x_prompt, x_sample, c_prompt, c_sample, cache_k, cache_v, cache_logf, state_ssm, state_conv, page_table, w_ada, b_ada, w_in, conv_w, gdn_a_log, gdn_dt_bias, gdn_norm_g, fox_f_bias, w_branch_a, w_branch_b, w_out, ln1_g, ln1_b, w_router, router_bias, w_gate, w_up, w_down, ws_gate, ws_up, ws_down, ln2_g, ln2_b):
    depth = w_ada.shape[0]
    bp, seq, d = x_prompt.shape
    ns, n_tok, _ = x_sample.shape
    n_pages = page_table.shape[1]
    page = cache_k.shape[2]
    heads_b, hd_b = cache_k.shape[3], cache_k.shape[4]
    width_b = heads_b * hd_b
    past = n_pages * page
    heads_a = gdn_a_log.shape[-1]
    conv_dim = conv_w.shape[-1]
    width_a = conv_dim // 3
    hd_a = width_a // heads_a
    assert hd_a * 2 == LANES and GDN_CHUNK == hd_a and seq % GDN_CHUNK == 0 and n_tok <= SUBLANES
    alpha = (2.0 * depth) ** 0.25
    g_heads = 4

    yp = x_prompt.reshape(bp * seq, d)
    ys = x_sample.reshape(ns * n_tok, d)
    outs = [[] for _ in range(10)]
    for l in range(depth):
        wts = dict(
            w_in=_prep_weights(w_in[l], conv_dim, width_a, heads_a, width_b, heads_b, d),
            conv_w=conv_w[l], a_log=gdn_a_log[l], dt_bias=gdn_dt_bias[l],
            norm_g=jnp.tile(gdn_norm_g[l], heads_a).reshape(1, width_a), f_bias=fox_f_bias[l],
            w_ba=w_branch_a[l].astype(BF16), w_bb=w_branch_b[l].astype(BF16), w_out=w_out[l].astype(BF16),
            ln1_g=ln1_g[l], ln1_b=ln1_b[l], w_router_t=w_router[l].T.astype(BF16), router_bias=router_bias[l],
            w_gu=jnp.concatenate([jnp.concatenate([w_gate[l], w_up[l]], axis=-1),
                                  jnp.concatenate([ws_gate[l], ws_up[l]], axis=-1)[None]], axis=0).astype(BF16),
            w_dn=jnp.concatenate([w_down[l], ws_down[l][None]], axis=0).astype(BF16),
            ln2_g=ln2_g[l], ln2_b=ln2_b[l],
            bsum=_block_ones(width_a, hd_a), bones=_block_ones(LANES, hd_a), esel=_expand_select(heads_a, hd_a))
        n_c = bp + ns
        c_all = jnp.concatenate([c_prompt, c_sample, jnp.zeros((-n_c % SUBLANES, d), F32)], axis=0)
        mod = _ada(c_all, w_ada[l], b_ada[l])
        mods_p = [m.reshape(bp, 1, d) for m in jnp.split(mod[:bp], 6, axis=-1)]
        mods_s = [jnp.repeat(m, n_tok, axis=0) for m in jnp.split(mod[bp:n_c], 6, axis=-1)]

        def fox_prompt(qb, kb, vb, logf):
            big_f = jnp.cumsum(logf.reshape(bp, seq, heads_b), axis=1) * LOG2E
            fg = big_f.reshape(bp, seq, heads_b // g_heads, g_heads)
            fq = jnp.pad(fg.transpose(0, 2, 1, 3), ((0, 0), (0, 0), (0, 0), (0, LANES - g_heads)))
            fk = jnp.pad(fg.transpose(0, 2, 3, 1), ((0, 0), (0, 0), (0, SUBLANES - g_heads), (0, 0)))
            o = _fox_prompt(qb.reshape(bp, seq, width_b), kb.reshape(bp, seq, width_b),
                            vb.reshape(bp, seq, width_b), fq, fk, g=g_heads, hd=hd_b, tq=128, tk=min(1024, seq))
            return o.reshape(bp * seq, width_b)

        yp, kp, vp, fp, sp, cp = _mixer_and_ffn(
            yp, mods_p, wts, seq_len=seq, hist=jnp.zeros((bp, SUBLANES, conv_dim), F32),
            ssm0=jnp.zeros((bp, heads_a, hd_a, hd_a), F32), fox_fn=fox_prompt, tm_proj=256, tm_moe=min(1024, seq),
            seq_rows=seq, conv_rows=256, gdn_pad_rows=seq, heads_a=heads_a, heads_b=heads_b, hd_b=hd_b,
            alpha=alpha)

        def fox_sample(qb, kb, vb, logf):
            fn = jnp.cumsum(logf.reshape(ns, n_tok, heads_b), axis=1) * LOG2E
            bq = fn.reshape(ns, n_tok * heads_b, 1)
            bnew = (fn[:, :, None, :] - fn[:, None, :, :]).transpose(0, 1, 3, 2).reshape(ns, n_tok * heads_b, n_tok)
            bnew = jnp.pad(bnew, ((0, 0), (0, 0), (0, LANES - n_tok)))
            lfp = jnp.take(cache_logf[l], page_table, axis=0).reshape(ns, past, heads_b)
            suffix = (lax.cumsum(lfp, axis=1, reverse=True) - lfp) * LOG2E
            bpast = suffix.transpose(0, 2, 1)
            pad_rows = lambda a: jnp.pad(a.reshape(ns, n_tok, width_b), ((0, 0), (0, LANES - n_tok), (0, 0)))
            o = _fox_sample(page_table, qb.reshape(ns, n_tok, width_b), pad_rows(kb), pad_rows(vb), bq, bnew,
                            bpast, cache_k[l].reshape(-1, page, width_b).astype(BF16),
                            cache_v[l].reshape(-1, page, width_b).astype(BF16),
                            n_heads=heads_b, hd=hd_b)
            return o.reshape(ns * n_tok, width_b)

        hist_s = jnp.concatenate([jnp.zeros((ns, SUBLANES - (CONV_WIDTH - 1), conv_dim), F32), state_conv[l]],
                                 axis=1)
        ys, ks, vs, fs, ss, cs = _mixer_and_ffn(
            ys, mods_s, wts, seq_len=n_tok, hist=hist_s,
            ssm0=state_ssm[l], fox_fn=fox_sample, tm_proj=256, tm_moe=ns * n_tok, seq_rows=2 * SUBLANES,
            conv_rows=min(256, ns * 2 * SUBLANES), gdn_pad_rows=GDN_CHUNK, heads_a=heads_a, heads_b=heads_b, hd_b=hd_b,
            alpha=alpha)

        new = (kp.reshape(bp, seq, heads_b, hd_b), vp.reshape(bp, seq, heads_b, hd_b),
               fp.reshape(bp, seq, heads_b), sp, cp,
               ks.reshape(ns, n_tok, heads_b, hd_b), vs.reshape(ns, n_tok, heads_b, hd_b),
               fs.reshape(ns, n_tok, heads_b), ss, cs)
        for lst, a in zip(outs, new):
            lst.append(a)
    return (yp.reshape(bp, seq, d), ys.reshape(ns, n_tok, d)) + tuple(jnp.stack(o) for o in outs)
```

```python
import functools

import jax
import jax.numpy as jnp
from jax import lax
from jax.experimental import pallas as pl
from jax.experimental.pallas import tpu as pltpu

F32 = jnp.float32
BF16 = jnp.bfloat16

V7X_VMEM_BYTES = 64 * 1024 * 1024
VMEM_LIMIT_BYTES = V7X_VMEM_BYTES - 8 * 1024 * 1024
LANES = 128
SUBLANES = 8

N_GROUPS = 8
TOPK_GROUPS = 4
TOP_K = 8
ROUTED_SCALE = 2.5
CONV_WIDTH = 4
GDN_CHUNK = 64
LN_EPS = 1e-5
RMS_EPS = 1e-6
LOG2E = 1.4426950408889634


def _cparams(*sem):
    return pltpu.CompilerParams(dimension_semantics=sem, vmem_limit_bytes=VMEM_LIMIT_BYTES)


def _dot(a, b):
    return jnp.dot(a, b, preferred_element_type=F32)


def _dot_nt(a, b):
    return lax.dot_general(a, b, (((1,), (1,)), ((), ())), preferred_element_type=F32)


def _dot_tn(a, b):
    return lax.dot_general(a, b, (((0,), (0,)), ((), ())), preferred_element_type=F32)


def _split2(x):
    hi = x.astype(BF16)
    lo = (x - hi.astype(F32)).astype(BF16)
    return hi, lo


def _split3(x):
    hi = x.astype(BF16)
    r = x - hi.astype(F32)
    mid = r.astype(BF16)
    lo = (r - mid.astype(F32)).astype(BF16)
    return hi, mid, lo


def _dot_sel(x, sel, parts):
    r = x.shape[0]
    pieces = _split3(x) if parts == 3 else _split2(x)
    y = _dot(jnp.concatenate(pieces, axis=0), sel)
    out = y[:r]
    for i in range(1, parts):
        out = out + y[i * r:(i + 1) * r]
    return out


def _block_diag(x, half):
    lane = lax.broadcasted_iota(jnp.int32, x.shape, 1)
    left = lane < half
    zero = jnp.zeros_like(x)
    return jnp.concatenate([jnp.where(left, x, zero), jnp.where(left, zero, x)], axis=0)


def _layernorm(r, g, b):
    mu = jnp.mean(r, axis=-1, keepdims=True)
    rc = r - mu
    var = jnp.mean(rc * rc, axis=-1, keepdims=True)
    return rc * lax.rsqrt(var + LN_EPS) * g + b


def _ada_kernel(c_ref, w_ref, b_ref, o_ref):
    o_ref[...] = _dot(c_ref[...].astype(BF16), w_ref[...].astype(BF16)) + b_ref[...]


def _ada(c, w, b, tn=512):
    m, d = c.shape
    n = w.shape[1]
    return pl.pallas_call(
        _ada_kernel,
        grid=(n // tn,),
        in_specs=[pl.BlockSpec((m, d), lambda j: (0, 0)),
                  pl.BlockSpec((d, tn), lambda j: (0, j)),
                  pl.BlockSpec((1, tn), lambda j: (0, j))],
        out_specs=pl.BlockSpec((m, tn), lambda j: (0, j)),
        out_shape=jax.ShapeDtypeStruct((m, n), F32),
        compiler_params=_cparams("arbitrary"),
        name="ada",
    )(c, w, b.reshape(1, n))


def _inproj_kernel(x_ref, sc_ref, sh_ref, w_ref, gates_ref, qkv_ref, z_ref, qb_ref, kb_ref, vb_ref,
                   kf_ref, vf_ref, small_ref, *, d, conv_dim, width, q_scale, per_row):
    sc = sc_ref[...] if per_row else sc_ref[0]
    sh = sh_ref[...] if per_row else sh_ref[0]
    h = (x_ref[...] * (1.0 + sc) + sh).astype(BF16)
    off = [0]

    def mm(n):
        a = off[0]
        off[0] = a + n
        return _dot(h, w_ref[:, a:a + n])

    gates_ref[...] = mm(2 * d)
    qkv_ref[...] = mm(conv_dim)
    z_ref[...] = mm(width)
    qb_ref[...] = (mm(width) * q_scale).astype(BF16)
    k = mm(width)
    kf_ref[...] = k
    kb_ref[...] = k.astype(BF16)
    v = mm(width)
    vf_ref[...] = v
    vb_ref[...] = v.astype(BF16)
    small_ref[...] = mm(LANES)


def _inproj(x, sc, sh, w_perm, *, seq_len, conv_dim, width, q_scale, tm):
    t, d = x.shape
    n = w_perm.shape[1]
    per_row = sc.ndim == 2
    if per_row:
        mod_spec = pl.BlockSpec((tm, d), lambda i: (i, 0))
    else:
        steps_per_seq = seq_len // tm
        mod_spec = pl.BlockSpec((1, 1, d), lambda i: (i // steps_per_seq, 0, 0))
    row = lambda w_: pl.BlockSpec((tm, w_), lambda i: (i, 0))
    outs = [(2 * d, F32), (conv_dim, F32), (width, F32), (width, BF16), (width, BF16), (width, BF16),
            (width, F32), (width, F32), (LANES, F32)]
    return pl.pallas_call(
        functools.partial(_inproj_kernel, d=d, conv_dim=conv_dim, width=width, q_scale=q_scale,
                          per_row=per_row),
        grid=(t // tm,),
        in_specs=[row(d), mod_spec, mod_spec,
                  pl.BlockSpec((d, n), lambda i: (0, 0), pipeline_mode=pl.Buffered(1))],
        out_specs=[row(w_) for w_, _ in outs],
        out_shape=[jax.ShapeDtypeStruct((t, w_), dt) for w_, dt in outs],
        compiler_params=_cparams("arbitrary"),
        name="inproj",
    )(x, sc, sh, w_perm)


def _conv_kernel(x_ref, prev_ref, hist_ref, cw_ref, bsum_ref, q_ref, k_ref, v_ref, xbuf, *, rows, width,
                 q_scale):
    t = pl.program_id(1)
    x = x_ref[...]
    halo = jnp.where(t == 0, hist_ref[0], prev_ref[...])
    xbuf[0:SUBLANES, :] = halo
    xbuf[SUBLANES:SUBLANES + rows, :] = x
    cw = cw_ref[...]
    conv = x * cw[3:4]
    for i in range(CONV_WIDTH - 1):
        shift = CONV_WIDTH - 1 - i
        conv = conv + xbuf[pl.ds(SUBLANES - shift, rows), :] * cw[i:i + 1]
    y = conv * jax.nn.sigmoid(conv)
    q = y[:, :width]
    k = y[:, width:2 * width]
    bsum = bsum_ref[...]
    q_ref[...] = q * lax.rsqrt(_dot_sel(q * q, bsum, 2) + RMS_EPS) * q_scale
    k_ref[...] = k * lax.rsqrt(_dot_sel(k * k, bsum, 2) + RMS_EPS)
    v_ref[...] = y[:, 2 * width:]


def _conv(x, hist, cw, bsum, *, seq_rows, rows, width, q_scale):
    t, cd = x.shape
    nb = t // seq_rows
    tiles = seq_rows // rows
    rb = rows // SUBLANES
    outs = [jax.ShapeDtypeStruct((t, width), F32)] * 3
    return pl.pallas_call(
        functools.partial(_conv_kernel, rows=rows, width=width, q_scale=q_scale),
        grid=(nb, tiles),
        in_specs=[pl.BlockSpec((rows, cd), lambda b, i: (b * tiles + i, 0)),
                  pl.BlockSpec((SUBLANES, cd), lambda b, i: (jnp.maximum((b * tiles + i) * rb - 1, 0), 0)),
                  pl.BlockSpec((1, SUBLANES, cd), lambda b, i: (b, 0, 0)),
                  pl.BlockSpec((CONV_WIDTH, cd), lambda b, i: (0, 0)),
                  pl.BlockSpec((width, width), lambda b, i: (0, 0))],
        out_specs=[pl.BlockSpec((rows, width), lambda b, i: (b * tiles + i, 0))] * 3,
        out_shape=outs,
        scratch_shapes=[pltpu.VMEM((rows + SUBLANES, cd), F32)],
        compiler_params=_cparams("arbitrary", "arbitrary"),
        name="gdn_conv",
    )(x, x, hist, cw, bsum)


def _sbs(a, b, half):
    return _dot(a.astype(BF16), _block_diag(b.astype(BF16), half))


def _gdn_prep_kernel(q_ref, k_ref, v_ref, aux_ref, grow_ref, esel_ref, w1_ref, w2_ref, qg_ref, kd_ref,
                     aqk_ref, gl_ref, beta_s, gc_s, *, chunk, n_chunks, width, n_live):
    half = LANES // 2
    ex = _dot_sel(aux_ref[...], esel_ref[...], 3)
    beta_s[...] = ex[:, :width]
    gc_s[...] = ex[:, width:]
    ii = lax.broadcasted_iota(jnp.int32, (chunk, LANES), 0)
    jj = lax.broadcasted_iota(jnp.int32, (chunk, LANES), 1) & (half - 1)
    incl = ii >= jj
    strict = ii > jj
    eye = (ii == jj).astype(F32)

    streams = [(c, p) for c in range(n_chunks) for p in range(width // LANES)]
    st = []
    for c, p in streams:
        rows = slice(c * chunk, (c + 1) * chunk)
        ls = slice(p * LANES, (p + 1) * LANES)
        kp, qp, vp = k_ref[rows, ls], q_ref[rows, ls], v_ref[rows, ls]
        bb, gb = beta_s[rows, ls], gc_s[rows, ls]
        dec = jnp.where(incl, jnp.exp(jnp.where(incl, gb - grow_ref[c:c + 1, p, :], 0.0)), 0.0)
        kb = kp.astype(BF16)
        kbd = _block_diag(kb, half)
        lm = jnp.where(strict, bb * _dot_nt(kb, kbd) * dec, 0.0)
        st.append(dict(kp=kp, qp=qp, vp=vp, bb=bb, gb=gb, dec=dec, lm=lm, qk=_dot_nt(qp.astype(BF16), kbd),
                       dinv=eye - jnp.where((ii ^ jj) == 1, lm, 0.0)))
    s = 2
    while s < min(chunk, n_live):
        sel = ((ii // (2 * s)) == (jj // (2 * s))) & ((ii & s) != 0) & ((jj & s) == 0)
        tmp = [_sbs(x["dinv"], jnp.where(sel, x["lm"], 0.0), half) for x in st]
        for x, t_ in zip(st, tmp):
            x["dinv"] = x["dinv"] - _sbs(t_, x["dinv"], half)
        s *= 2
    for (c, p), x in zip(streams, st):
        rows = slice(c * chunk, (c + 1) * chunk)
        ls = slice(p * LANES, (p + 1) * LANES)
        gb, kp, bb = x["gb"], x["kp"], x["bb"]
        gam = jnp.exp(gb)
        rhs = jnp.concatenate([_block_diag((bb * x["vp"]).astype(BF16), half),
                               _block_diag((bb * gam * kp).astype(BF16), half)], axis=1)
        w = _dot(x["dinv"].astype(BF16), rhs)
        glast = gb[chunk - 1:chunk, :]
        w1_ref[rows, ls] = w[:, :LANES]
        w2_ref[rows, ls] = w[:, LANES:].astype(BF16)
        qg_ref[rows, ls] = (x["qp"] * gam).astype(BF16)
        kd_ref[rows, ls] = (kp * jnp.exp(glast - gb)).astype(BF16)
        aqk_ref[rows, ls] = (x["qk"] * x["dec"]).astype(BF16)
        gl_ref[c, :, ls] = jnp.exp(glast)


def _gdn_prep(q, k, v, aux, grow, esel, *, chunk, n_chunks, n_live):
    t, width = q.shape
    rows = chunk * n_chunks
    nct = t // chunk
    row = lambda dt: (pl.BlockSpec((rows, width), lambda i: (i, 0)), jax.ShapeDtypeStruct((t, width), dt))
    outs = [row(F32), row(BF16), row(BF16), row(BF16), row(BF16),
            (pl.BlockSpec((n_chunks, 1, width), lambda i: (i, 0, 0)),
             jax.ShapeDtypeStruct((nct, 1, width), F32))]
    return pl.pallas_call(
        functools.partial(_gdn_prep_kernel, chunk=chunk, n_chunks=n_chunks, width=width, n_live=n_live),
        grid=(t // rows,),
        in_specs=[pl.BlockSpec((rows, width), lambda i: (i, 0))] * 3 + [
            pl.BlockSpec((rows, LANES), lambda i: (i, 0)),
            pl.BlockSpec((n_chunks, width // LANES, LANES), lambda i: (i, 0, 0)),
            pl.BlockSpec((LANES, 2 * width), lambda i: (0, 0))],
        out_specs=[o[0] for o in outs],
        out_shape=[o[1] for o in outs],
        scratch_shapes=[pltpu.VMEM((rows, width), F32), pltpu.VMEM((rows, width), F32)],
        compiler_params=_cparams("arbitrary"),
        name="gdn_prep",
    )(q, k, v, aux, grow, esel)


def _gdn_state_kernel(w1_ref, w2_ref, qg_ref, kd_ref, aqk_ref, gl_ref, z_ref, ng_ref, s0_ref, bones_ref,
                      o_ref, sout_ref, s_sc, *, chunk, n_chunks, n_seq, width, hd):
    half = LANES // 2
    n_pairs = width // LANES
    c = pl.program_id(1)
    streams = [(b, p) for b in range(n_seq) for p in range(n_pairs)]

    @pl.when(c == 0)
    def _():
        for i, (b, p) in enumerate(streams):
            pair = jnp.concatenate([s0_ref[b, 2 * p], s0_ref[b, 2 * p + 1]], axis=1)
            s_sc[i] = _block_diag(pair, half)

    r_i = lax.broadcasted_iota(jnp.int32, (LANES, LANES), 0)
    c_i = lax.broadcasted_iota(jnp.int32, (LANES, LANES), 1)
    same_head = (r_i < half) == (c_i < half)
    bones = bones_ref[...]
    for cc in range(n_chunks):
        rows = slice(cc * chunk, (cc + 1) * chunk)
        ins = []
        for i, (b, p) in enumerate(streams):
            ls = slice(p * LANES, (p + 1) * LANES)
            ins.append((s_sc[i], w1_ref[b, rows, ls], w2_ref[b, rows, ls], qg_ref[b, rows, ls],
                        kd_ref[b, rows, ls], aqk_ref[b, rows, ls], gl_ref[b, cc, :, ls], z_ref[b, rows, ls]))
        outs = []
        for (b, p), (s, w1, w2, qg, kd, aqk, gl, zz) in zip(streams, ins):
            ls = slice(p * LANES, (p + 1) * LANES)
            sb = s.astype(BF16)
            u = w1 - _dot(w2, sb)
            ub = u.astype(BF16)
            o = _dot(qg, sb) + _dot(aqk, _block_diag(ub, half))
            s_new = jnp.where(same_head, gl * s + _dot_tn(kd, ub), 0.0)
            ms = _dot_sel(o * o, bones, 2) * (1.0 / hd)
            on = o * lax.rsqrt(ms + RMS_EPS) * ng_ref[:, ls] * (zz * jax.nn.sigmoid(zz))
            outs.append((s_new, on.astype(BF16)))
        for i, (b, p) in enumerate(streams):
            s_sc[i] = outs[i][0]
            o_ref[b, rows, p * LANES:(p + 1) * LANES] = outs[i][1]

    @pl.when(c == pl.num_programs(1) - 1)
    def _():
        for i, (b, p) in enumerate(streams):
            s = s_sc[i]
            sout_ref[b, 2 * p] = s[:half, :half]
            sout_ref[b, 2 * p + 1] = s[half:, half:]


def _gdn_state(w1, w2, qg, kd, aqk, gl, z, ng, s0, bones, *, chunk, n_chunks, n_seq, seq_rows, hd):
    t, width = w1.shape
    nb = t // seq_rows
    rows = chunk * n_chunks
    nh = s0.shape[1]
    as_seq = lambda a: a.reshape(nb, seq_rows, width)
    row = pl.BlockSpec((n_seq, rows, width), lambda b, i: (b, i, 0))
    state = pl.BlockSpec((n_seq, nh, hd, hd), lambda b, i: (b, 0, 0, 0))
    o, s_out = pl.pallas_call(
        functools.partial(_gdn_state_kernel, chunk=chunk, n_chunks=n_chunks, n_seq=n_seq, width=width, hd=hd),
        grid=(nb // n_seq, seq_rows // rows),
        in_specs=[row, row, row, row, row,
                  pl.BlockSpec((n_seq, n_chunks, 1, width), lambda b, i: (b, i, 0, 0)),
                  row,
                  pl.BlockSpec((1, width), lambda b, i: (0, 0)),
                  state,
                  pl.BlockSpec((LANES, LANES), lambda b, i: (0, 0))],
        out_specs=[row, state],
        out_shape=[jax.ShapeDtypeStruct((nb, seq_rows, width), BF16), jax.ShapeDtypeStruct(s0.shape, F32)],
        scratch_shapes=[pltpu.VMEM((n_seq * (width // LANES), LANES, LANES), F32)],
        compiler_params=_cparams("arbitrary", "arbitrary"),
        name="gdn_state",
    )(as_seq(w1), as_seq(w2), as_seq(qg), as_seq(kd), as_seq(aqk),
      gl.reshape(nb, seq_rows // chunk, 1, width), as_seq(z), ng, s0, bones)
    return o.reshape(t, width), s_out


def _fox_prompt_kernel(q_ref, k_ref, v_ref, fq_ref, fk_ref, o_ref, m_sc, l_sc, acc_sc, *, g, ng, hd, tq, tk):
    qi = pl.program_id(1)
    gw = g * hd
    lane = lax.broadcasted_iota(jnp.int32, (tq, gw), 1) // hd
    q_bd, fq_col = [], []
    for gi in range(ng):
        q = q_ref[0, :, gi * gw:(gi + 1) * gw]
        zero = jnp.zeros_like(q)
        q_bd.append(jnp.concatenate([jnp.where(lane == i, q, zero) for i in range(g)], axis=0))
        fq = fq_ref[0, gi]
        fq_col.append(jnp.concatenate([fq[:, i:i + 1] for i in range(g)], axis=0))
    m_sc[...] = jnp.full(m_sc.shape, -jnp.inf, F32)
    l_sc[...] = jnp.zeros(l_sc.shape, F32)
    acc_sc[...] = jnp.zeros(acc_sc.shape, F32)

    def block(j, masked):
        ks = pl.ds(pl.multiple_of(j * tk, tk), tk)
        if masked:
            rr = lax.broadcasted_iota(jnp.int32, (tq, tk), 0) + qi * tq
            cc = lax.broadcasted_iota(jnp.int32, (tq, tk), 1) + j * tk
            ok = jnp.concatenate([rr >= cc] * g, axis=0)
        old = [(m_sc[gi], l_sc[gi], acc_sc[gi]) for gi in range(ng)]
        new = []
        for gi in range(ng):
            kb = k_ref[0, ks, gi * gw:(gi + 1) * gw]
            vb = v_ref[0, ks, gi * gw:(gi + 1) * gw]
            fk = fk_ref[0, gi, :, ks]
            fk_full = jnp.concatenate([jnp.broadcast_to(fk[i:i + 1, :], (tq, tk)) for i in range(g)], axis=0)
            s = _dot_nt(q_bd[gi], kb) + fq_col[gi] - fk_full
            if masked:
                s = jnp.where(ok, s, -jnp.inf)
            m_old, l_old, acc_old = old[gi]
            m_new = jnp.maximum(m_old, jnp.max(s, axis=1, keepdims=True))
            alpha = jnp.exp2(m_old - m_new)
            p = jnp.exp2(s - m_new)
            new.append((m_new, alpha * l_old + jnp.sum(p, axis=1, keepdims=True),
                        alpha * acc_old + _dot(p.astype(BF16), vb)))
        for gi in range(ng):
            m_sc[gi], l_sc[gi], acc_sc[gi] = new[gi]

    jd = (qi * tq) // tk

    def full_body(j, carry):
        block(j, False)
        return carry

    lax.fori_loop(0, jd, full_body, 0)
    block(jd, True)
    for gi in range(ng):
        acc = acc_sc[gi] / l_sc[gi]
        out = jnp.zeros((tq, gw), F32)
        for i in range(g):
            out = jnp.where(lane == i, acc[i * tq:(i + 1) * tq], out)
        o_ref[0, :, gi * gw:(gi + 1) * gw] = out.astype(BF16)


def _fox_prompt(q, k, v, fq, fk, *, g, hd, tq, tk):
    b, s, w = q.shape
    gw = g * hd
    ng = w // gw
    return pl.pallas_call(
        functools.partial(_fox_prompt_kernel, g=g, ng=ng, hd=hd, tq=tq, tk=tk),
        grid=(b, s // tq),
        in_specs=[pl.BlockSpec((1, tq, w), lambda bi, qi: (bi, qi, 0)),
                  pl.BlockSpec((1, s, w), lambda bi, qi: (bi, 0, 0)),
                  pl.BlockSpec((1, s, w), lambda bi, qi: (bi, 0, 0)),
                  pl.BlockSpec((1, ng, tq, LANES), lambda bi, qi: (bi, 0, qi, 0)),
                  pl.BlockSpec((1, ng, SUBLANES, s), lambda bi, qi: (bi, 0, 0, 0))],
        out_specs=pl.BlockSpec((1, tq, w), lambda bi, qi: (bi, qi, 0)),
        out_shape=jax.ShapeDtypeStruct((b, s, w), BF16),
        scratch_shapes=[pltpu.VMEM((ng, g * tq, 1), F32), pltpu.VMEM((ng, g * tq, 1), F32),
                        pltpu.VMEM((ng, g * tq, gw), F32)],
        compiler_params=_cparams("arbitrary", "arbitrary"),
        name="fox_prompt",
    )(q, k, v, fq, fk)


def _fox_sample_kernel(pt_ref, q_ref, kn_ref, vn_ref, bq_ref, bnew_ref, bpast_ref, ck_hbm, cv_hbm, o_ref,
                       kbuf, vbuf, sem, *, n_pages, page, n_tok, n_heads, hd):
    b = pl.program_id(0)
    nb = pl.num_programs(0)
    slot = b % 2

    def copies(bb, sl):
        out = []
        for j in range(n_pages):
            pg = pt_ref[bb, j]
            out.append(pltpu.make_async_copy(ck_hbm.at[pg], kbuf.at[sl, j], sem.at[sl, 0]))
            out.append(pltpu.make_async_copy(cv_hbm.at[pg], vbuf.at[sl, j], sem.at[sl, 1]))
        return out

    @pl.when(b == 0)
    def _():
        for c in copies(0, 0):
            c.start()

    @pl.when(b + 1 < nb)
    def _():
        for c in copies(b + 1, 1 - slot):
            c.start()

    for c in copies(b, slot):
        c.wait()

    w = n_heads * hd
    q = q_ref[0].astype(F32)
    sub = lax.broadcasted_iota(jnp.int32, (n_heads, w), 0)
    lane_head = lax.broadcasted_iota(jnp.int32, (n_heads, w), 1) // hd
    mine = sub == lane_head
    q_bd = jnp.concatenate(
        [jnp.where(mine, jnp.broadcast_to(q[l:l + 1, :], (n_heads, w)), 0.0) for l in range(n_tok)],
        axis=0).astype(BF16)
    past = n_pages * page
    kp = kbuf[slot].reshape(past, n_heads, hd).reshape(past, w).astype(BF16)
    vp = vbuf[slot].reshape(past, n_heads, hd).reshape(past, w).astype(BF16)
    bq = bq_ref[0]
    s_past = _dot_nt(q_bd, kp) + bq + jnp.concatenate([bpast_ref[0]] * n_tok, axis=0)
    kn = kn_ref[0]
    vn = vn_ref[0]
    s_new = _dot_nt(q_bd, kn) + bnew_ref[0]
    r_tok = lax.broadcasted_iota(jnp.int32, s_new.shape, 0) // n_heads
    c_tok = lax.broadcasted_iota(jnp.int32, s_new.shape, 1)
    s_new = jnp.where(c_tok <= r_tok, s_new, -jnp.inf)
    m = jnp.maximum(jnp.max(s_past, axis=1, keepdims=True), jnp.max(s_new, axis=1, keepdims=True))
    p_past = jnp.exp2(s_past - m)
    p_new = jnp.exp2(s_new - m)
    denom = jnp.sum(p_past, axis=1, keepdims=True) + jnp.sum(p_new, axis=1, keepdims=True)
    o_all = (_dot(p_past.astype(BF16), vp) + _dot(p_new.astype(BF16), vn)) / denom
    rows = [jnp.sum(jnp.where(mine, o_all[l * n_heads:(l + 1) * n_heads], 0.0), axis=0, keepdims=True)
            for l in range(n_tok)]
    o_ref[0] = jnp.concatenate(rows, axis=0).astype(BF16)


def _fox_sample(page_table, q, kn, vn, bq, bnew, bpast, cache_k, cache_v, *, n_heads, hd):
    ns, n_tok, w = q.shape
    n_pages = page_table.shape[1]
    page = cache_k.shape[1]
    past = n_pages * page
    lh = n_tok * n_heads
    grid_spec = pltpu.PrefetchScalarGridSpec(
        num_scalar_prefetch=1,
        grid=(ns,),
        in_specs=[pl.BlockSpec((1, n_tok, w), lambda b, pt: (b, 0, 0)),
                  pl.BlockSpec((1, LANES, w), lambda b, pt: (b, 0, 0)),
                  pl.BlockSpec((1, LANES, w), lambda b, pt: (b, 0, 0)),
                  pl.BlockSpec((1, lh, 1), lambda b, pt: (b, 0, 0)),
                  pl.BlockSpec((1, lh, LANES), lambda b, pt: (b, 0, 0)),
                  pl.BlockSpec((1, n_heads, past), lambda b, pt: (b, 0, 0)),
                  pl.BlockSpec(memory_space=pl.ANY),
                  pl.BlockSpec(memory_space=pl.ANY)],
        out_specs=pl.BlockSpec((1, n_tok, w), lambda b, pt: (b, 0, 0)),
        scratch_shapes=[pltpu.VMEM((2, n_pages, page, n_heads, hd), cache_k.dtype),
                        pltpu.VMEM((2, n_pages, page, n_heads, hd), cache_v.dtype),
                        pltpu.SemaphoreType.DMA((2, 2))],
    )
    return pl.pallas_call(
        functools.partial(_fox_sample_kernel, n_pages=n_pages, page=page, n_tok=n_tok, n_heads=n_heads,
                          hd=hd),
        grid_spec=grid_spec,
        out_shape=jax.ShapeDtypeStruct((ns, n_tok, w), BF16),
        compiler_params=_cparams("arbitrary"),
        name="fox_sample",
    )(page_table, q, kn, vn, bq, bnew, bpast, cache_k, cache_v)


def _post_kernel(oa_ref, ob_ref, gates_ref, x_ref, gt_ref, sc_ref, sh_ref, wba_ref, wbb_ref, wout_ref, g_ref,
                 b_ref, x1_ref, h2_ref, *, alpha, per_row):
    gt = gt_ref[...] if per_row else gt_ref[0]
    sc = sc_ref[...] if per_row else sc_ref[0]
    sh = sh_ref[...] if per_row else sh_ref[0]
    ya = _dot(oa_ref[...], wba_ref[...])
    yb = _dot(ob_ref[...], wbb_ref[...])
    d = ya.shape[1]
    gts = gates_ref[...]
    merged = jax.nn.sigmoid(gts[:, :d]) * ya + jax.nn.sigmoid(gts[:, d:]) * yb
    mo = _dot(merged.astype(BF16), wout_ref[...])
    x1 = _layernorm(alpha * x_ref[...] + gt * mo, g_ref[...], b_ref[...])
    x1_ref[...] = x1
    h2_ref[...] = (x1 * (1.0 + sc) + sh).astype(BF16)


def _post(oa, ob, gates, x, gt, sc, sh, wba, wbb, wout, g, b, *, alpha, seq_len, tm):
    t, d = x.shape
    width = oa.shape[1]
    per_row = gt.ndim == 2
    if per_row:
        mod_spec = pl.BlockSpec((tm, d), lambda i: (i, 0))
    else:
        steps_per_seq = seq_len // tm
        mod_spec = pl.BlockSpec((1, 1, d), lambda i: (i // steps_per_seq, 0, 0))
    row = lambda w_: pl.BlockSpec((tm, w_), lambda i: (i, 0))
    const = lambda r, c: pl.BlockSpec((r, c), lambda i: (0, 0), pipeline_mode=pl.Buffered(1))
    return pl.pallas_call(
        functools.partial(_post_kernel, alpha=alpha, per_row=per_row),
        grid=(t // tm,),
        in_specs=[row(width), row(width), row(2 * d), row(d), mod_spec, mod_spec, mod_spec,
                  const(width, d), const(width, d), const(d, d), const(1, d), const(1, d)],
        out_specs=[row(d), row(d)],
        out_shape=[jax.ShapeDtypeStruct((t, d), F32), jax.ShapeDtypeStruct((t, d), BF16)],
        compiler_params=_cparams("arbitrary"),
        name="post_mixer",
    )(oa, ob, gates, x, gt, sc, sh, wba, wbb, wout, g.reshape(1, d), b.reshape(1, d))


def _router_kernel(h_ref, wrt_ref, bias_ref, gates_ref, *, n_groups, topk_groups, top_k, scale):
    logits = _dot_nt(wrt_ref[...], h_ref[...])
    s = jax.nn.sigmoid(logits)
    sb = s + bias_ref[...]
    n_e, tm = s.shape
    gs = n_e // n_groups
    eidx = lax.broadcasted_iota(jnp.int32, (n_e, tm), 0)
    sub = lax.broadcasted_iota(jnp.int32, (gs, tm), 0)
    neg = -jnp.inf
    gscores = []
    for gi in range(n_groups):
        slab = sb[gi * gs:(gi + 1) * gs]
        m1 = jnp.max(slab, axis=0, keepdims=True)
        first = jnp.min(jnp.where(slab == m1, sub, gs), axis=0, keepdims=True)
        m2 = jnp.max(jnp.where(sub == first, neg, slab), axis=0, keepdims=True)
        gscores.append(m1 + m2)
    cur = jnp.concatenate(gscores, axis=0)
    gidx = lax.broadcasted_iota(jnp.int32, cur.shape, 0)
    gsel = jnp.zeros(cur.shape, jnp.bool_)
    for _ in range(topk_groups):
        mx = jnp.max(cur, axis=0, keepdims=True)
        first = jnp.min(jnp.where(cur == mx, gidx, n_groups), axis=0, keepdims=True)
        hit = gidx == first
        gsel = jnp.logical_or(gsel, hit)
        cur = jnp.where(hit, neg, cur)
    emask = jnp.concatenate([jnp.broadcast_to(gsel[gi:gi + 1], (gs, tm)) for gi in range(n_groups)], axis=0)
    cur = jnp.where(emask, sb, neg)
    sel = jnp.zeros(cur.shape, jnp.bool_)
    for _ in range(top_k):
        mx = jnp.max(cur, axis=0, keepdims=True)
        first = jnp.min(jnp.where(cur == mx, eidx, n_e), axis=0, keepdims=True)
        hit = eidx == first
        sel = jnp.logical_or(sel, hit)
        cur = jnp.where(hit, neg, cur)
    wsel = jnp.where(sel, s, 0.0)
    gates_ref[...] = wsel / jnp.sum(wsel, axis=0, keepdims=True) * scale


def _router(h2, wrt, bias, *, tm):
    t, d = h2.shape
    n_e = wrt.shape[0]
    return pl.pallas_call(
        functools.partial(_router_kernel, n_groups=N_GROUPS, topk_groups=TOPK_GROUPS, top_k=TOP_K,
                          scale=ROUTED_SCALE),
        grid=(t // tm,),
        in_specs=[pl.BlockSpec((tm, d), lambda i: (i, 0)),
                  pl.BlockSpec((n_e, d), lambda i: (0, 0)),
                  pl.BlockSpec((n_e, 1), lambda i: (0, 0))],
        out_specs=pl.BlockSpec((n_e, tm), lambda i: (0, i)),
        out_shape=jax.ShapeDtypeStruct((n_e, t), F32),
        compiler_params=_cparams("arbitrary"),
        name="router",
    )(h2, wrt, bias.reshape(n_e, 1))


def _moe_kernel(h_ref, gates_ref, wgu_ref, wd_ref, x1_ref, gt_ref, g_ref, b_ref, y_ref, acc_ref, *, eb, ff,
                alpha, per_row):
    e = pl.program_id(1)

    @pl.when(e == 0)
    def _():
        acc_ref[...] = jnp.zeros(acc_ref.shape, F32)

    h = h_ref[...]
    gts = gates_ref[...]
    lane = lax.broadcasted_iota(jnp.int32, gts.shape, 1)
    hs = []
    for i in range(eb):
        gu = _dot(h, wgu_ref[i])
        gate = gu[:, :ff]
        act = gate * jax.nn.sigmoid(gate) * gu[:, ff:]
        gcol = jnp.sum(jnp.where(lane == e * eb + i, gts, 0.0), axis=1, keepdims=True)
        hs.append((act * gcol).astype(BF16))
    hcat = jnp.concatenate(hs, axis=1)
    d = acc_ref.shape[1]
    acc_ref[...] += _dot(hcat, wd_ref[...].reshape(eb * ff, d))

    @pl.when(e == pl.num_programs(1) - 1)
    def _():
        gt = gt_ref[...] if per_row else gt_ref[0]
        y_ref[...] = _layernorm(alpha * x1_ref[...] + gt * acc_ref[...], g_ref[...], b_ref[...])


def _moe(h2, gates, wgu, wd, x1, gt, g, b, *, alpha, seq_len, tm, eb):
    t, d = h2.shape
    n_e, _, ff2 = wgu.shape
    ff = ff2 // 2
    per_row = gt.ndim == 2
    if per_row:
        mod_spec = pl.BlockSpec((tm, d), lambda i, e: (i, 0))
    else:
        steps_per_seq = seq_len // tm
        mod_spec = pl.BlockSpec((1, 1, d), lambda i, e: (i // steps_per_seq, 0, 0))
    return pl.pallas_call(
        functools.partial(_moe_kernel, eb=eb, ff=ff, alpha=alpha, per_row=per_row),
        grid=(t // tm, n_e // eb),
        in_specs=[pl.BlockSpec((tm, d), lambda i, e: (i, 0)),
                  pl.BlockSpec((tm, LANES), lambda i, e: (i, 0)),
                  pl.BlockSpec((eb, d, ff2), lambda i, e: (e, 0, 0)),
                  pl.BlockSpec((eb, ff, d), lambda i, e: (e, 0, 0)),
                  pl.BlockSpec((tm, d), lambda i, e: (i, 0)),
                  mod_spec,
                  pl.BlockSpec((1, d), lambda i, e: (0, 0)),
                  pl.BlockSpec((1, d), lambda i, e: (0, 0))],
        out_specs=pl.BlockSpec((tm, d), lambda i, e: (i, 0)),
        out_shape=jax.ShapeDtypeStruct((t, d), F32),
        scratch_shapes=[pltpu.VMEM((tm, d), F32)],
        compiler_params=_cparams("arbitrary", "arbitrary"),
        name="moe",
    )(h2, gates, wgu, wd, x1, gt, g.reshape(1, d), b.reshape(1, d))


def _block_ones(n, blk):
    i = jnp.arange(n) // blk
    return (i[:, None] == i[None, :]).astype(BF16)


def _expand_select(n_heads, hd):
    ch = jnp.arange(LANES)[:, None]
    col = jnp.arange(2 * n_heads * hd)[None, :]
    return (ch == col // hd).astype(BF16)


def _prep_weights(w_in, conv_dim, width_a, heads_a, width_b, heads_b, d):
    sizes = (conv_dim, width_a, heads_a, heads_a, width_b, width_b, width_b, heads_b, d, d)
    cuts, acc = [], 0
    for s in sizes[:-1]:
        acc += s
        cuts.append(acc)
    qkv_a, z_a, b_a, a_a, q_b, k_b, v_b, f_b, g_a, g_b = jnp.split(w_in, cuts, axis=-1)
    pad = jnp.zeros((d, LANES - 2 * heads_a - heads_b), w_in.dtype)
    return jnp.concatenate([g_a, g_b, qkv_a, z_a, q_b, k_b, v_b, b_a, a_a, f_b, pad], axis=1).astype(BF16)


def _mixer_and_ffn(x, mods, wts, *, seq_len, hist, ssm0, fox_fn, tm_proj, tm_moe, seq_rows, conv_rows,
                   gdn_pad_rows, heads_a, heads_b, hd_b, alpha):
    t, d = x.shape
    nb = t // seq_len
    sh1, sc1, gt1, sh2, sc2, gt2 = mods
    conv_dim = wts["conv_w"].shape[-1]
    width_a = conv_dim // 3
    hd_a = width_a // heads_a
    width_b = heads_b * hd_b
    gates, qkv_a, z_a, qb, kb, vb, kf, vf, small = _inproj(
        x, sc1, sh1, wts["w_in"], seq_len=seq_len, conv_dim=conv_dim, width=width_a, q_scale=hd_b ** -0.5 * LOG2E,
        tm=tm_proj)
    beta = jax.nn.sigmoid(small[:, :heads_a])
    g = -jnp.exp(wts["a_log"]) * jax.nn.softplus(small[:, heads_a:2 * heads_a] + wts["dt_bias"])
    logf = jax.nn.log_sigmoid(small[:, 2 * heads_a:2 * heads_a + heads_b] + wts["f_bias"])

    xa = qkv_a.reshape(nb, seq_len, conv_dim)
    conv_new = jnp.concatenate([hist[:, SUBLANES - (CONV_WIDTH - 1):], xa], axis=1)[:, seq_len:]
    if seq_rows == seq_len:
        first = 0
        qa, ka, va = _conv(xa.reshape(t, conv_dim), hist, wts["conv_w"], wts["bsum"], seq_rows=seq_len,
                           rows=conv_rows, width=width_a, q_scale=hd_a ** -0.5)
    else:
        first = SUBLANES
        stream = jnp.concatenate([hist, xa, jnp.zeros((nb, seq_rows - SUBLANES - seq_len, conv_dim), F32)], axis=1)
        qa, ka, va = _conv(stream.reshape(nb * seq_rows, conv_dim), jnp.zeros((1, SUBLANES, conv_dim), F32),
                           wts["conv_w"], wts["bsum"], seq_rows=nb * seq_rows, rows=conv_rows, width=width_a,
                           q_scale=hd_a ** -0.5)

    def seq_pad(a, rows_in, start):
        a = a.reshape(nb, rows_in, a.shape[-1])[:, start:start + seq_len]
        if gdn_pad_rows != seq_len:
            a = jnp.pad(a, ((0, 0), (0, gdn_pad_rows - seq_len), (0, 0)))
        return a.reshape(nb * gdn_pad_rows, a.shape[-1])

    qa, ka, va = (seq_pad(a, seq_rows, first) for a in (qa, ka, va))
    zp = seq_pad(z_a, seq_len, 0)
    beta_p = seq_pad(beta, seq_len, 0)
    g_p = seq_pad(g, seq_len, 0)
    n_ct = nb * gdn_pad_rows // GDN_CHUNK
    gc = jnp.cumsum(g_p.reshape(n_ct, GDN_CHUNK, heads_a), axis=1)
    aux = jnp.concatenate([beta_p, gc.reshape(-1, heads_a),
                           jnp.zeros((nb * gdn_pad_rows, LANES - 2 * heads_a), F32)], axis=1)
    grow = gc.reshape(n_ct, GDN_CHUNK, heads_a // 2, 2).transpose(0, 2, 3, 1).reshape(n_ct, heads_a // 2, LANES)
    w1, w2, qg, kd, aqk, gl = _gdn_prep(qa, ka, va, aux, grow, wts["esel"], chunk=GDN_CHUNK, n_chunks=4,
                                        n_live=min(seq_len, GDN_CHUNK))
    oa, ssm_new = _gdn_state(w1, w2, qg, kd, aqk, gl, zp, wts["norm_g"], ssm0, wts["bones"], chunk=GDN_CHUNK,
                             n_chunks=min(2, gdn_pad_rows // GDN_CHUNK), n_seq=min(4, nb),
                             seq_rows=gdn_pad_rows, hd=hd_a)
    if gdn_pad_rows != seq_len:
        oa = oa.reshape(nb, gdn_pad_rows, width_a)[:, :seq_len].reshape(t, width_a)

    ob = fox_fn(qb, kb, vb, logf)

    x1, h2 = _post(oa, ob, gates, x, gt1, sc2, sh2, wts["w_ba"], wts["w_bb"], wts["w_out"], wts["ln1_g"],
                   wts["ln1_b"], alpha=alpha, seq_len=seq_len, tm=tm_proj)
    gates_t = _router(h2, wts["w_router_t"], wts["router_bias"], tm=min(512, t))
    n_e = gates_t.shape[0]
    gates_m = jnp.concatenate([gates_t.T, jnp.ones((t, 1), F32), jnp.zeros((t, LANES - n_e - 1), F32)], axis=1)
    y = _moe(h2, gates_m, wts["w_gu"], wts["w_dn"], x1, gt2, wts["ln2_g"], wts["ln2_b"], alpha=alpha,
             seq_len=seq_len, tm=tm_moe, eb=5)
    return y, kf, vf, logf, ssm_new, conv_new


def kernel(x_prompt, x_sample, c_prompt, c_sample, cache_k, cache_v, cache_logf, state_ssm, state_conv, page_table, w_ada, b_ada, w_in, conv_w, gdn_a_log, gdn_dt_bias, gdn_norm_g, fox_f_bias, w_branch_a, w_branch_b, w_out, ln1_g, ln1_b, w_router, router_bias, w_gate, w_up, w_down, ws_gate, ws_up, ws_down, ln2_g, ln2_b):
    depth = w_ada.shape[0]
    bp, seq, d = x_prompt.shape
    ns, n_tok, _ = x_sample.shape
    n_pages = page_table.shape[1]
    page = cache_k.shape[2]
    heads_b, hd_b = cache_k.shape[3], cache_k.shape[4]
    width_b = heads_b * hd_b
    past = n_pages * page
    heads_a = gdn_a_log.shape[-1]
    conv_dim = conv_w.shape[-1]
    width_a = conv_dim // 3
    hd_a = width_a // heads_a
    assert hd_a * 2 == LANES and GDN_CHUNK == hd_a and seq % GDN_CHUNK == 0 and n_tok <= SUBLANES
    alpha = (2.0 * depth) ** 0.25
    g_heads = 4

    yp = x_prompt.reshape(bp * seq, d)
    ys = x_sample.reshape(ns * n_tok, d)
    outs = [[] for _ in range(10)]
    for l in range(depth):
        wts = dict(
            w_in=_prep_weights(w_in[l], conv_dim, width_a, heads_a, width_b, heads_b, d),
            conv_w=conv_w[l], a_log=gdn_a_log[l], dt_bias=gdn_dt_bias[l],
            norm_g=jnp.tile(gdn_norm_g[l], heads_a).reshape(1, width_a), f_bias=fox_f_bias[l],
            w_ba=w_branch_a[l].astype(BF16), w_bb=w_branch_b[l].astype(BF16), w_out=w_out[l].astype(BF16),
            ln1_g=ln1_g[l], ln1_b=ln1_b[l], w_router_t=w_router[l].T.astype(BF16), router_bias=router_bias[l],
            w_gu=jnp.concatenate([jnp.concatenate([w_gate[l], w_up[l]], axis=-1),
                                  jnp.concatenate([ws_gate[l], ws_up[l]], axis=-1)[None]], axis=0).astype(BF16),
            w_dn=jnp.concatenate([w_down[l], ws_down[l][None]], axis=0).astype(BF16),
            ln2_g=ln2_g[l], ln2_b=ln2_b[l],
            bsum=_block_ones(width_a, hd_a), bones=_block_ones(LANES, hd_a), esel=_expand_select(heads_a, hd_a))
        n_c = bp + ns
        c_all = jnp.concatenate([c_prompt, c_sample, jnp.zeros((-n_c % SUBLANES, d), F32)], axis=0)
        mod = _ada(c_all, w_ada[l], b_ada[l])
        mods_p = [m.reshape(bp, 1, d) for m in jnp.split(mod[:bp], 6, axis=-1)]
        mods_s = [jnp.repeat(m, n_tok, axis=0) for m in jnp.split(mod[bp:n_c], 6, axis=-1)]

        def fox_prompt(qb, kb, vb, logf):
            big_f = jnp.cumsum(logf.reshape(bp, seq, heads_b), axis=1) * LOG2E
            fg = big_f.reshape(bp, seq, heads_b // g_heads, g_heads)
            fq = jnp.pad(fg.transpose(0, 2, 1, 3), ((0, 0), (0, 0), (0, 0), (0, LANES - g_heads)))
            fk = jnp.pad(fg.transpose(0, 2, 3, 1), ((0, 0), (0, 0), (0, SUBLANES - g_heads), (0, 0)))
            o = _fox_prompt(qb.reshape(bp, seq, width_b), kb.reshape(bp, seq, width_b),
                            vb.reshape(bp, seq, width_b), fq, fk, g=g_heads, hd=hd_b, tq=128, tk=min(1024, seq))
            return o.reshape(bp * seq, width_b)

        yp, kp, vp, fp, sp, cp = _mixer_and_ffn(
            yp, mods_p, wts, seq_len=seq, hist=jnp.zeros((bp, SUBLANES, conv_dim), F32),
            ssm0=jnp.zeros((bp, heads_a, hd_a, hd_a), F32), fox_fn=fox_prompt, tm_proj=256, tm_moe=min(1024, seq),
            seq_rows=seq, conv_rows=256, gdn_pad_rows=seq, heads_a=heads_a, heads_b=heads_b, hd_b=hd_b,
            alpha=alpha)

        def fox_sample(qb, kb, vb, logf):
            fn = jnp.cumsum(logf.reshape(ns, n_tok, heads_b), axis=1) * LOG2E
            bq = fn.reshape(ns, n_tok * heads_b, 1)
            bnew = (fn[:, :, None, :] - fn[:, None, :, :]).transpose(0, 1, 3, 2).reshape(ns, n_tok * heads_b, n_tok)
            bnew = jnp.pad(bnew, ((0, 0), (0, 0), (0, LANES - n_tok)))
            lfp = jnp.take(cache_logf[l], page_table, axis=0).reshape(ns, past, heads_b)
            suffix = (lax.cumsum(lfp, axis=1, reverse=True) - lfp) * LOG2E
            bpast = suffix.transpose(0, 2, 1)
            pad_rows = lambda a: jnp.pad(a.reshape(ns, n_tok, width_b), ((0, 0), (0, LANES - n_tok), (0, 0)))
            o = _fox_sample(page_table, qb.reshape(ns, n_tok, width_b), pad_rows(kb), pad_rows(vb), bq, bnew,
                            bpast, cache_k[l], cache_v[l],
                            n_heads=heads_b, hd=hd_b)
            return o.reshape(ns * n_tok, width_b)

        hist_s = jnp.concatenate([jnp.zeros((ns, SUBLANES - (CONV_WIDTH - 1), conv_dim), F32), state_conv[l]],
                                 axis=1)
        ys, ks, vs, fs, ss, cs = _mixer_and_ffn(
            ys, mods_s, wts, seq_len=n_tok, hist=hist_s,
            ssm0=state_ssm[l], fox_fn=fox_sample, tm_proj=256, tm_moe=ns * n_tok, seq_rows=2 * SUBLANES,
            conv_rows=min(256, ns * 2 * SUBLANES), gdn_pad_rows=GDN_CHUNK, heads_a=heads_a, heads_b=heads_b, hd_b=hd_b,
            alpha=alpha)

        new = (kp.reshape(bp, seq, heads_b, hd_b), vp.reshape(bp, seq, heads_b, hd_b),
               fp.reshape(bp, seq, heads_b), sp, cp,
               ks.reshape(ns, n_tok, heads_b, hd_b), vs.reshape(ns, n_tok, heads_b, hd_b),
               fs.reshape(ns, n_tok, heads_b), ss, cs)
        for lst, a in zip(outs, new):
            lst.append(a)
    return (yp.reshape(bp, seq, d), ys.reshape(ns, n_tok, d)) + tuple(jnp.stack(o) for o in outs)
```
